```python
import math
import jax, jax.numpy as jnp
from jax import lax
import numpy as np

D_MODEL = 2048
BATCH = 16
SEQ = 2048
DEPTH = 4

N_META = 16
GRID_W = 64
Q_BLOCK = 128
ROPE_THETA = 10000.0
EPS = 1e-6

MLA_HEADS = 8
Q_LORA = 512
KV_LORA = 256
NOPE_DIM = 128
ROPE_DIM = 64
V_DIM = 128
QK_HEAD = NOPE_DIM + ROPE_DIM
MLA_WIDTH = MLA_HEADS * V_DIM
MLA_SCALE = 1.0 / math.sqrt(QK_HEAD)

GQA_HEADS = 8
GQA_KV_HEADS = 2
GQA_HEAD_DIM = 128
GQA_WIDTH = GQA_HEADS * GQA_HEAD_DIM
GQA_SCALE = 1.0 / math.sqrt(GQA_HEAD_DIM)

MIX_WIDTH = MLA_WIDTH + GQA_WIDTH

OFF_CQ = 0
OFF_CKV = OFF_CQ + Q_LORA
OFF_KR = OFF_CKV + KV_LORA
OFF_GQ = OFF_KR + ROPE_DIM
OFF_GK = OFF_GQ + GQA_HEADS * GQA_HEAD_DIM
OFF_GV = OFF_GK + GQA_KV_HEADS * GQA_HEAD_DIM
IN_COLS = OFF_GV + GQA_KV_HEADS * GQA_HEAD_DIM

N_EXPERTS = 16
N_GROUPS = 4
EXPERTS_PER_GROUP = N_EXPERTS // N_GROUPS
TOP_K = 2
EXPERT_FF = 1024

ALPHA = (2.0 * DEPTH) ** 0.25
BETA = (8.0 * DEPTH) ** -0.25

kernel_name = "hybrid_mla_gqa_axial_grouped_moe_deepnorm_encoder"


def layer_norm(x, g, b):
    xf = x.astype(jnp.float32)
    mu = jnp.mean(xf, axis=-1, keepdims=True)
    xc = xf - mu
    var = jnp.mean(xc * xc, axis=-1, keepdims=True)
    y = xc * lax.rsqrt(var + EPS) * g.astype(jnp.float32) + b.astype(jnp.float32)
    return y.astype(x.dtype)


def rms_norm(x, g):
    xf = x.astype(jnp.float32)
    y = xf * lax.rsqrt(jnp.mean(xf * xf, axis=-1, keepdims=True) + EPS) * g.astype(jnp.float32)
    return y.astype(x.dtype)


def axial_rope_tables(pos_row, pos_col, rot_dim):
    n = rot_dim // 4
    inv = ROPE_THETA ** (-jnp.arange(n, dtype=jnp.float32) / n)
    ang = jnp.concatenate([pos_row[:, None] * inv, pos_col[:, None] * inv], axis=-1)
    return jnp.cos(ang), jnp.sin(ang)


def apply_rope(x, cos, sin):
    half = x.shape[-1] // 2
    x1, x2 = x[..., :half], x[..., half:]
    c = cos[None, :, None, :].astype(x.dtype)
    s = sin[None, :, None, :].astype(x.dtype)
    return jnp.concatenate([x1 * c - x2 * s, x1 * s + x2 * c], axis=-1)


def bidir_attention(q, k, v, scale):
    B, T, Hq, dk = q.shape
    Hk, dv = k.shape[2], v.shape[-1]
    G = Hq // Hk
    S = T - N_META
    qg = q.reshape(B, T, Hk, G, dk)

    def attend(qb):
        s = jnp.einsum('bqhgd,bkhd->bhgqk', qb, k).astype(jnp.float32) * scale
        p = jax.nn.softmax(s, axis=-1).astype(v.dtype)
        return jnp.einsum('bhgqk,bkhd->bqhgd', p, v)

    out_meta = attend(qg[:, :N_META])
    nb = S // Q_BLOCK
    q_real = qg[:, N_META:].reshape(B, nb, Q_BLOCK, Hk, G, dk).transpose(1, 0, 2, 3, 4, 5)
    out_real = lax.map(attend, q_real)
    out_real = out_real.transpose(1, 0, 2, 3, 4, 5).reshape(B, S, Hk, G, dv)
    out = jnp.concatenate([out_meta, out_real], axis=1)
    return out.reshape(B, T, Hq, dv)


def grouped_moe(h, w_router, router_bias, w_gate, w_up, w_down):
    B, T, D = h.shape
    t = h.reshape(B * T, D)
    scores = jax.nn.sigmoid((t @ w_router).astype(jnp.float32))
    sel = scores + router_bias.astype(jnp.float32)
    grp_score = lax.top_k(sel.reshape(-1, N_GROUPS, EXPERTS_PER_GROUP), 2)[0].sum(-1)
    g_idx = jnp.argmax(grp_score, axis=-1)
    in_group = (jnp.arange(N_EXPERTS) // EXPERTS_PER_GROUP)[None, :] == g_idx[:, None]
    masked = jnp.where(in_group, sel, -jnp.inf)
    _, e_idx = lax.top_k(masked, TOP_K)
    w = jnp.take_along_axis(scores, e_idx, axis=-1)
    w = w / jnp.sum(w, axis=-1, keepdims=True)
    combine = jnp.sum(jax.nn.one_hot(e_idx, N_EXPERTS, dtype=jnp.float32) * w[..., None], axis=1)
    combine = combine.astype(t.dtype)
    y = jnp.zeros_like(t)
    for e in range(N_EXPERTS):
        he = jax.nn.silu(t @ w_gate[e]) * (t @ w_up[e])
        y = y + combine[:, e:e + 1] * (he @ w_down[e])
    return y.reshape(B, T, D)


def setup_inputs(seed: int = 0) -> dict:
    key = jax.random.key(seed)
    ks = jax.random.split(key, 23)

    def nrm(k, shape, scale):
        return jax.random.normal(k, shape, jnp.float32) * scale

    def gain(k, shape):
        return 1.0 + 0.05 * jax.random.normal(k, shape, jnp.float32)

    return {
        "x": nrm(ks[0], (BATCH, SEQ, D_MODEL), 1.0),
        "meta_tokens": nrm(ks[1], (N_META, D_MODEL), 1.0),
        "ln_in_g": gain(ks[2], (D_MODEL,)),
        "ln_in_b": nrm(ks[3], (D_MODEL,), 0.02),
        "w_in": nrm(ks[4], (DEPTH, D_MODEL, IN_COLS), D_MODEL ** -0.5),
        "g_q_lora": gain(ks[5], (DEPTH, Q_LORA)),
        "w_q_b": nrm(ks[6], (DEPTH, Q_LORA, MLA_HEADS * QK_HEAD), Q_LORA ** -0.5),
        "g_kv_lora": gain(ks[7], (DEPTH, KV_LORA)),
        "w_kv_b": nrm(ks[8], (DEPTH, KV_LORA, MLA_HEADS * (NOPE_DIM + V_DIM)), KV_LORA ** -0.5),
        "g_qk_q": gain(ks[9], (DEPTH, GQA_HEAD_DIM)),
        "g_qk_k": gain(ks[10], (DEPTH, GQA_HEAD_DIM)),
        "g_out_mla": gain(ks[11], (DEPTH, MLA_WIDTH)),
        "g_out_gqa": gain(ks[12], (DEPTH, GQA_WIDTH)),
        "w_out": nrm(ks[13], (DEPTH, MIX_WIDTH, D_MODEL), MIX_WIDTH ** -0.5 * BETA),
        "ln1_g": gain(ks[14], (DEPTH, D_MODEL)),
        "ln1_b": nrm(ks[15], (DEPTH, D_MODEL), 0.02),
        "w_router": nrm(ks[16], (D_MODEL, N_EXPERTS), D_MODEL ** -0.5),
        "router_bias": nrm(ks[17], (N_EXPERTS,), 0.01),
        "w_gate": nrm(ks[18], (DEPTH, N_EXPERTS, D_MODEL, EXPERT_FF), D_MODEL ** -0.5),
        "w_up": nrm(ks[19], (DEPTH, N_EXPERTS, D_MODEL, EXPERT_FF), D_MODEL ** -0.5),
        "w_down": nrm(ks[20], (DEPTH, N_EXPERTS, EXPERT_FF, D_MODEL), EXPERT_FF ** -0.5 * BETA),
        "ln2_g": gain(ks[21], (DEPTH, D_MODEL)),
        "ln2_b": nrm(ks[22], (DEPTH, D_MODEL), 0.02),
    }


def reference(x, meta_tokens, ln_in_g, ln_in_b, w_in, g_q_lora, w_q_b, g_kv_lora, w_kv_b,
              g_qk_q, g_qk_k, g_out_mla, g_out_gqa, w_out, ln1_g, ln1_b,
              w_router, router_bias, w_gate, w_up, w_down, ln2_g, ln2_b):
    B, S, D = x.shape
    T = N_META + S
    ROWS = S // GRID_W

    pos_row = jnp.concatenate([jnp.full((N_META,), -1.0, jnp.float32),
                               jnp.repeat(jnp.arange(ROWS, dtype=jnp.float32), GRID_W)])
    pos_col = jnp.concatenate([jnp.arange(N_META, dtype=jnp.float32),
                               jnp.tile(jnp.arange(GRID_W, dtype=jnp.float32), ROWS)])
    cos_a, sin_a = axial_rope_tables(pos_row, pos_col, ROPE_DIM)
    cos_b, sin_b = axial_rope_tables(pos_row, pos_col, GQA_HEAD_DIM)

    meta = jnp.broadcast_to(meta_tokens[None].astype(x.dtype), (B, N_META, D))
    h = layer_norm(jnp.concatenate([meta, x], axis=1), ln_in_g, ln_in_b)

    for l in range(DEPTH):
        u = h @ w_in[l]

        c_q = rms_norm(u[..., OFF_CQ:OFF_CKV], g_q_lora[l])
        q_a = (c_q @ w_q_b[l]).reshape(B, T, MLA_HEADS, QK_HEAD)
        q_nope, q_rope = q_a[..., :NOPE_DIM], apply_rope(q_a[..., NOPE_DIM:], cos_a, sin_a)
        c_kv = rms_norm(u[..., OFF_CKV:OFF_KR], g_kv_lora[l])
        kv = (c_kv @ w_kv_b[l]).reshape(B, T, MLA_HEADS, NOPE_DIM + V_DIM)
        k_nope, v_a = kv[..., :NOPE_DIM], kv[..., NOPE_DIM:]
        k_rope = apply_rope(u[..., OFF_KR:OFF_GQ].reshape(B, T, 1, ROPE_DIM), cos_a, sin_a)
        q_full = jnp.concatenate([q_nope, q_rope], axis=-1)
        k_full = jnp.concatenate([k_nope, jnp.broadcast_to(k_rope, (B, T, MLA_HEADS, ROPE_DIM))], axis=-1)
        o_a = bidir_attention(q_full, k_full, v_a, MLA_SCALE).reshape(B, T, MLA_WIDTH)

        q_b = rms_norm(u[..., OFF_GQ:OFF_GK].reshape(B, T, GQA_HEADS, GQA_HEAD_DIM), g_qk_q[l])
        k_b = rms_norm(u[..., OFF_GK:OFF_GV].reshape(B, T, GQA_KV_HEADS, GQA_HEAD_DIM), g_qk_k[l])
        v_b = u[..., OFF_GV:IN_COLS].reshape(B, T, GQA_KV_HEADS, GQA_HEAD_DIM)
        q_b = apply_rope(q_b, cos_b, sin_b)
        k_b = apply_rope(k_b, cos_b, sin_b)
        o_b = bidir_attention(q_b, k_b, v_b, GQA_SCALE).reshape(B, T, GQA_WIDTH)

        mixed = jnp.concatenate([rms_norm(o_a, g_out_mla[l]), rms_norm(o_b, g_out_gqa[l])], axis=-1) @ w_out[l]
        h = layer_norm(ALPHA * h + mixed, ln1_g[l], ln1_b[l])

        ffn = grouped_moe(h, w_router, router_bias, w_gate[l], w_up[l], w_down[l])
        h = layer_norm(ALPHA * h + ffn, ln2_g[l], ln2_b[l])

    return h[:, N_META:]
```

```python
import functools
import math

import jax
import jax.numpy as jnp
from jax import lax
from jax.experimental import pallas as pl
from jax.experimental.pallas import tpu as pltpu

N_META = 16
GRID_W = 64
ROPE_THETA = 10000.0
EPS = 1e-6

MLA_HEADS = 8
Q_LORA = 512
KV_LORA = 256
NOPE_DIM = 128
ROPE_DIM = 64
V_DIM = 128
QK_HEAD = NOPE_DIM + ROPE_DIM
MLA_WIDTH = MLA_HEADS * V_DIM
MLA_SCALE = 1.0 / math.sqrt(QK_HEAD)
MLA_QK_PAD = 256

GQA_HEADS = 8
GQA_KV_HEADS = 2
GQA_HEAD_DIM = 128
GQA_WIDTH = GQA_HEADS * GQA_HEAD_DIM
GQA_SCALE = 1.0 / math.sqrt(GQA_HEAD_DIM)

N_EXPERTS = 16
N_GROUPS = 4
EXPERTS_PER_GROUP = 4
PAIRS_PER_GROUP = 6
N_CLASSES = N_GROUPS * PAIRS_PER_GROUP
EXPERT_FF = 1024

LANES = 128
VMEM_LIMIT_BYTES = 58 * 1024 * 1024
EXPERT_TILE = 256
NEG_BIG = -1e30

C_CQ = 0
C_CKV = C_CQ + Q_LORA
C_KR = C_CKV + KV_LORA
C_GQ = C_KR + LANES
C_GK = C_GQ + GQA_WIDTH
C_GV = C_GK + GQA_KV_HEADS * GQA_HEAD_DIM
IN_COLS_PAD = C_GV + GQA_KV_HEADS * GQA_HEAD_DIM


def _cparams(semantics):
    return pltpu.CompilerParams(dimension_semantics=semantics, vmem_limit_bytes=VMEM_LIMIT_BYTES)


def _const_spec(block_shape, index_map):
    return pl.BlockSpec(block_shape, index_map, pipeline_mode=pl.Buffered(1))


def _row_tile(n, cap, mult=16):
    best = mult
    for t in range(mult, cap + 1, mult):
        if n % t == 0:
            best = t
    assert n % best == 0
    return best


def _layer_norm_rows(z, g, b):
    mu = jnp.mean(z, axis=-1, keepdims=True)
    zc = z - mu
    var = jnp.mean(zc * zc, axis=-1, keepdims=True)
    return zc * lax.rsqrt(var + EPS) * g + b


def _rms_rows(z, g):
    return z * lax.rsqrt(jnp.mean(z * z, axis=-1, keepdims=True) + EPS) * g


def _dot(a, b):
    return jnp.dot(a, b, preferred_element_type=jnp.float32)


def _ln_in_kernel(x_ref, g_ref, b_ref, h_ref, hb_ref):
    y = _layer_norm_rows(x_ref[...], g_ref[...], b_ref[...])
    h_ref[...] = y
    hb_ref[...] = y.astype(jnp.bfloat16)


def _ln_in(xin, g, b):
    n, d = xin.shape
    tm = _row_tile(n, 1024)
    return pl.pallas_call(
        _ln_in_kernel,
        grid=(n // tm,),
        in_specs=[
            pl.BlockSpec((tm, d), lambda i: (i, 0)),
            pl.BlockSpec((1, d), lambda i: (0, 0)),
            pl.BlockSpec((1, d), lambda i: (0, 0)),
        ],
        out_specs=[pl.BlockSpec((tm, d), lambda i: (i, 0)), pl.BlockSpec((tm, d), lambda i: (i, 0))],
        out_shape=[jax.ShapeDtypeStruct((n, d), jnp.float32), jax.ShapeDtypeStruct((n, d), jnp.bfloat16)],
        compiler_params=_cparams(("parallel",)),
        name="ln_in",
    )(xin, g, b)


def _swap_halves_64(x):
    lane = lax.broadcasted_iota(jnp.int32, x.shape, 1)
    fwd = pltpu.roll(x, 32, 1)
    bwd = pltpu.roll(x, 96, 1)
    return jnp.where((lane & 63) < 32, bwd, fwd)


def _in_proj_kernel(x_ref, w_in_ref, wq_ref, wkv_ref, gq_ref, gkv_ref, gqq_ref, gqk_ref,
                    cos_a_ref, sin_a_ref, cos_b_ref, sin_b_ref,
                    q_mla_ref, k_mla_ref, v_mla_ref, q_gqa_ref, k_gqa_ref, v_gqa_ref):
    bf16 = jnp.bfloat16
    u = _dot(x_ref[...], w_in_ref[0])
    cos_a, sin_a = cos_a_ref[...], sin_a_ref[...]
    cos_b, sin_b = cos_b_ref[...], sin_b_ref[...]
    lane = lax.broadcasted_iota(jnp.int32, cos_a.shape, 1)

    c_q = _rms_rows(u[:, C_CQ:C_CKV], gq_ref[0]).astype(bf16)
    qa = _dot(c_q, wq_ref[0])
    c_kv = _rms_rows(u[:, C_CKV:C_KR], gkv_ref[0]).astype(bf16)
    kv = _dot(c_kv, wkv_ref[0])
    kr = u[:, C_KR:C_GQ]
    kr = kr * cos_a + _swap_halves_64(kr) * sin_a
    rope_base = MLA_HEADS * NOPE_DIM
    for pair in range(MLA_HEADS // 2):
        blk = qa[:, rope_base + LANES * pair:rope_base + LANES * (pair + 1)]
        rot = blk * cos_a + _swap_halves_64(blk) * sin_a
        for half in range(2):
            h = 2 * pair + half
            own = jnp.where((lane >> 6) == half, rot, 0.0)
            qh = jnp.concatenate([qa[:, NOPE_DIM * h:NOPE_DIM * (h + 1)], own], axis=1) * MLA_SCALE
            q_mla_ref[0, h] = qh.astype(bf16)
    for h in range(MLA_HEADS):
        kh = jnp.concatenate([kv[:, NOPE_DIM * h:NOPE_DIM * (h + 1)], kr], axis=1)
        k_mla_ref[0, h] = kh.astype(bf16)
        v0 = MLA_HEADS * NOPE_DIM + V_DIM * h
        v_mla_ref[0, h] = kv[:, v0:v0 + V_DIM].astype(bf16)

    for h in range(GQA_HEADS):
        qh = _rms_rows(u[:, C_GQ + GQA_HEAD_DIM * h:C_GQ + GQA_HEAD_DIM * (h + 1)], gqq_ref[0])
        qh = qh * cos_b + pltpu.roll(qh, 64, 1) * sin_b
        q_gqa_ref[0, h] = (qh * GQA_SCALE).astype(bf16)
    for h in range(GQA_KV_HEADS):
        kh = _rms_rows(u[:, C_GK + GQA_HEAD_DIM * h:C_GK + GQA_HEAD_DIM * (h + 1)], gqk_ref[0])
        kh = kh * cos_b + pltpu.roll(kh, 64, 1) * sin_b
        k_gqa_ref[0, h] = kh.astype(bf16)
        v_gqa_ref[0, h] = u[:, C_GV + GQA_HEAD_DIM * h:C_GV + GQA_HEAD_DIM * (h + 1)].astype(bf16)


def _in_proj(hb, w_in_p, wq_p, wkv_p, g_q, g_kv, g_qq, g_qk, tabs, layer, batch, tp):
    n, d = hb.shape
    tm = _row_tile(tp, 320)
    nt = tp // tm
    cos_a, sin_a, cos_b, sin_b = tabs
    wmap = lambda b, i: (layer, 0, 0)
    tmap = lambda b, i: (i, 0)
    omap = lambda b, i: (b, 0, i, 0)
    bf16 = jnp.bfloat16
    return pl.pallas_call(
        _in_proj_kernel,
        grid=(batch, nt),
        in_specs=[
            pl.BlockSpec((tm, d), lambda b, i: (b * nt + i, 0)),
            _const_spec((1, d, IN_COLS_PAD), wmap),
            _const_spec((1, Q_LORA, wq_p.shape[2]), wmap),
            _const_spec((1, KV_LORA, wkv_p.shape[2]), wmap),
            _const_spec((1, 1, Q_LORA), wmap),
            _const_spec((1, 1, KV_LORA), wmap),
            _const_spec((1, 1, GQA_HEAD_DIM), wmap),
            _const_spec((1, 1, GQA_HEAD_DIM), wmap),
            pl.BlockSpec((tm, LANES), tmap),
            pl.BlockSpec((tm, LANES), tmap),
            pl.BlockSpec((tm, LANES), tmap),
            pl.BlockSpec((tm, LANES), tmap),
        ],
        out_specs=[
            pl.BlockSpec((1, MLA_HEADS, tm, MLA_QK_PAD), omap),
            pl.BlockSpec((1, MLA_HEADS, tm, MLA_QK_PAD), omap),
            pl.BlockSpec((1, MLA_HEADS, tm, V_DIM), omap),
            pl.BlockSpec((1, GQA_HEADS, tm, GQA_HEAD_DIM), omap),
            pl.BlockSpec((1, GQA_KV_HEADS, tm, GQA_HEAD_DIM), omap),
            pl.BlockSpec((1, GQA_KV_HEADS, tm, GQA_HEAD_DIM), omap),
        ],
        out_shape=[
            jax.ShapeDtypeStruct((batch, MLA_HEADS, tp, MLA_QK_PAD), bf16),
            jax.ShapeDtypeStruct((batch, MLA_HEADS, tp, MLA_QK_PAD), bf16),
            jax.ShapeDtypeStruct((batch, MLA_HEADS, tp, V_DIM), bf16),
            jax.ShapeDtypeStruct((batch, GQA_HEADS, tp, GQA_HEAD_DIM), bf16),
            jax.ShapeDtypeStruct((batch, GQA_KV_HEADS, tp, GQA_HEAD_DIM), bf16),
            jax.ShapeDtypeStruct((batch, GQA_KV_HEADS, tp, GQA_HEAD_DIM), bf16),
        ],
        compiler_params=_cparams(("parallel", "parallel")),
        name="in_proj",
    )(hb, w_in_p, wq_p, wkv_p, g_q, g_kv, g_qq, g_qk, cos_a, sin_a, cos_b, sin_b)


def _attn_kernel(q_ref, k_ref, v_ref, o_ref, *, tq, n_keys):
    k = k_ref[0, 0]
    v = v_ref[0, 0]
    tp = k.shape[0]
    key_ok = lax.broadcasted_iota(jnp.int32, (1, tp), 1) < n_keys
    for c in range(tp // tq):
        q = q_ref[0, 0, c * tq:(c + 1) * tq, :]
        s = lax.dot_general(q, k, (((1,), (1,)), ((), ())), preferred_element_type=jnp.float32)
        s = jnp.where(key_ok, s, NEG_BIG)
        m = jnp.max(s, axis=-1, keepdims=True)
        p = jnp.exp(s - m)
        l = jnp.sum(p, axis=-1, keepdims=True)
        o = _dot(p.astype(jnp.bfloat16), v)
        o_ref[c * tq:(c + 1) * tq, :] = (o / l).astype(o_ref.dtype)


def _attention(q, k, v, n_keys):
    batch, hq, tp, dk = q.shape
    hk, dv = k.shape[1], v.shape[3]
    rep = hq // hk
    tq = _row_tile(tp, 640)
    return pl.pallas_call(
        functools.partial(_attn_kernel, tq=tq, n_keys=n_keys),
        grid=(batch, hq),
        in_specs=[
            pl.BlockSpec((1, 1, tp, dk), lambda b, h: (b, h, 0, 0)),
            pl.BlockSpec((1, 1, tp, dk), lambda b, h: (b, h // rep, 0, 0)),
            pl.BlockSpec((1, 1, tp, dv), lambda b, h: (b, h // rep, 0, 0)),
        ],
        out_specs=pl.BlockSpec((tp, dv), lambda b, h: (b, h)),
        out_shape=jax.ShapeDtypeStruct((batch * tp, hq * dv), jnp.bfloat16),
        compiler_params=_cparams(("parallel", "parallel")),
        name="attention",
    )(q, k, v)


def _first_index_of_max(vals, lane_f):
    m = jnp.max(vals, axis=-1, keepdims=True)
    idx = jnp.min(jnp.where(vals == m, lane_f, float(LANES)), axis=-1, keepdims=True)
    return m, idx


def _out_proj_kernel(oa_ref, ob_ref, h_ref, w_out_ref, ga_ref, gb_ref, lg_ref, lb_ref,
                     wr_hi_ref, wr_lo_ref, rbias_ref,
                     h1_ref, ri_ref, rw_ref, cnt_ref, carry_ref, *, alpha, tp, n_tok):
    bf16 = jnp.bfloat16
    i = pl.program_id(0)
    tm = h_ref.shape[0]

    @pl.when(i == 0)
    def _():
        carry_ref[...] = jnp.zeros_like(carry_ref)

    na = _rms_rows(oa_ref[...].astype(jnp.float32), ga_ref[0]).astype(bf16)
    nb = _rms_rows(ob_ref[...].astype(jnp.float32), gb_ref[0]).astype(bf16)
    mixed = _dot(na, w_out_ref[0, :MLA_WIDTH, :]) + _dot(nb, w_out_ref[0, MLA_WIDTH:, :])
    h1 = _layer_norm_rows(alpha * h_ref[...] + mixed, lg_ref[0], lb_ref[0])
    h1_ref[...] = h1

    hi = h1.astype(bf16)
    lo = (h1 - hi.astype(jnp.float32)).astype(bf16)
    logits = _dot(hi, wr_hi_ref[...]) + _dot(hi, wr_lo_ref[...]) + _dot(lo, wr_hi_ref[...])
    scores = jax.nn.sigmoid(logits)

    lane = lax.broadcasted_iota(jnp.int32, (tm, LANES), 1)
    lane_f = lane.astype(jnp.float32)
    neg = -jnp.inf
    sel = jnp.where(lane < N_EXPERTS, scores + rbias_ref[...], neg)
    grp = lane >> 2

    best = None
    for g in range(N_GROUPS):
        mg = jnp.where(grp == g, sel, neg)
        m1, i1 = _first_index_of_max(mg, lane_f)
        m2 = jnp.max(jnp.where(lane_f == i1, neg, mg), axis=-1, keepdims=True)
        gs = m1 + m2
        if best is None:
            best, gi = gs, jnp.zeros_like(gs)
        else:
            better = gs > best
            gi = jnp.where(better, float(g), gi)
            best = jnp.where(better, gs, best)

    mg = jnp.where(grp.astype(jnp.float32) == gi, sel, neg)
    _, e1 = _first_index_of_max(mg, lane_f)
    mg2 = jnp.where(lane_f == e1, neg, mg)
    _, e2 = _first_index_of_max(mg2, lane_f)
    w1 = jnp.sum(jnp.where(lane_f == e1, scores, 0.0), axis=-1, keepdims=True)
    w2 = jnp.sum(jnp.where(lane_f == e2, scores, 0.0), axis=-1, keepdims=True)
    den = w1 + w2
    w1, w2 = w1 / den, w2 / den

    first_lower = e1 < e2
    la = jnp.where(first_lower, e1, e2) - EXPERTS_PER_GROUP * gi
    lb = jnp.where(first_lower, e2, e1) - EXPERTS_PER_GROUP * gi
    w_a = jnp.where(first_lower, w1, w2)
    w_b = jnp.where(first_lower, w2, w1)
    cls = PAIRS_PER_GROUP * gi + la * (7.0 - la) * 0.5 + (lb - la - 1.0)

    row = (i * tm + lax.broadcasted_iota(jnp.int32, (tm, 1), 0)).astype(jnp.float32)
    routed = (row - jnp.floor((row + 0.5) * (1.0 / tp)) * tp) < n_tok
    onehot = jnp.where((lane_f == cls) & routed, 1.0, 0.0)

    r_i = lax.broadcasted_iota(jnp.int32, (tm, tm), 0)
    c_i = lax.broadcasted_iota(jnp.int32, (tm, tm), 1)
    lower = jnp.where(c_i < r_i, 1.0, 0.0).astype(bf16)
    before = _dot(lower, onehot.astype(bf16)) + carry_ref[...]
    rank = jnp.sum(onehot * before, axis=-1, keepdims=True)
    carry_ref[...] += jnp.sum(onehot, axis=0, keepdims=True)
    cnt_ref[...] = carry_ref[...]

    cls_out = jnp.where(routed, cls, -1.0)
    ri_ref[...] = jnp.where(lane == 0, cls_out, jnp.where(lane == 1, rank, 0.0)).astype(jnp.int32)
    rw_ref[...] = jnp.where(lane == 0, w_a, jnp.where(lane == 1, w_b, 0.0))


def _out_proj(o_a, o_b, h, w_out_b, g_a, g_b, ln_g, ln_b, wr_hi, wr_lo, rbias, layer, alpha, tp, n_tok):
    n, d = h.shape
    tm = _row_tile(n, 256)
    wmap = lambda i: (layer, 0, 0)
    cmap = lambda i: (0, 0)
    rmap = lambda i: (i, 0)
    return pl.pallas_call(
        functools.partial(_out_proj_kernel, alpha=alpha, tp=tp, n_tok=n_tok),
        grid=(n // tm,),
        in_specs=[
            pl.BlockSpec((tm, MLA_WIDTH), rmap),
            pl.BlockSpec((tm, GQA_WIDTH), rmap),
            pl.BlockSpec((tm, d), rmap),
            _const_spec((1, MLA_WIDTH + GQA_WIDTH, d), wmap),
            _const_spec((1, 1, MLA_WIDTH), wmap),
            _const_spec((1, 1, GQA_WIDTH), wmap),
            _const_spec((1, 1, d), wmap),
            _const_spec((1, 1, d), wmap),
            _const_spec((d, LANES), cmap),
            _const_spec((d, LANES), cmap),
            _const_spec((1, LANES), cmap),
        ],
        out_specs=[
            pl.BlockSpec((tm, d), rmap),
            pl.BlockSpec((tm, LANES), rmap),
            pl.BlockSpec((tm, LANES), rmap),
            pl.BlockSpec((1, LANES), cmap),
        ],
        out_shape=[
            jax.ShapeDtypeStruct((n, d), jnp.float32),
            jax.ShapeDtypeStruct((n, LANES), jnp.int32),
            jax.ShapeDtypeStruct((n, LANES), jnp.float32),
            jax.ShapeDtypeStruct((1, LANES), jnp.float32),
        ],
        scratch_shapes=[pltpu.VMEM((1, LANES), jnp.float32)],
        compiler_params=_cparams(("arbitrary",)),
        name="out_proj",
    )(o_a, o_b, h, w_out_b, g_a, g_b, ln_g, ln_b, wr_hi, wr_lo, rbias)


def _gather_rows(idx_ref, src_hbm, dst_ref, sem, n_rows):
    def issue(r, carry):
        t = idx_ref[0, 0, r]
        pltpu.make_async_copy(src_hbm.at[pl.ds(t, 1), :], dst_ref.at[pl.ds(r, 1), :], sem).start()
        return carry

    lax.fori_loop(0, n_rows, issue, 0, unroll=8)
    pltpu.make_async_copy(src_hbm.at[pl.ds(0, n_rows), :], dst_ref, sem).wait()


def _experts_kernel(tile_e_ref, meta_ref, src_ref, h1_hbm, wab_ref, wg_ref, wu_ref, wd_ref,
                    ys_ref, xrow_ref, xb_ref, acc_ref, sem):
    i = pl.program_id(0)
    k = pl.program_id(1)
    tm = xrow_ref.shape[0]

    @pl.when(i < meta_ref[0])
    def _():
        @pl.when(k == 0)
        def _():
            _gather_rows(src_ref, h1_hbm, xrow_ref, sem, tm)
            xb_ref[...] = xrow_ref[...].astype(jnp.bfloat16)

        x = xb_ref[...]
        g = _dot(x, wg_ref[0, 0])
        u = _dot(x, wu_ref[0, 0])
        a = (g * jax.nn.sigmoid(g) * u).astype(jnp.bfloat16)
        y = _dot(a, wd_ref[0, 0])
        second = (k + i) % 2 == 1
        w = jnp.where(second, wab_ref[:, 1:2], wab_ref[:, 0:1])
        y = y * w

        @pl.when(k == 0)
        def _():
            acc_ref[...] = y

        @pl.when(k == 1)
        def _():
            ys_ref[...] = acc_ref[...] + y

    @pl.when((i >= meta_ref[0]) & (k == 1))
    def _():
        ys_ref[...] = jnp.zeros_like(ys_ref)


def _experts(tile_e, meta, src3, h1, wab, wg_b, wu_b, wd_b, layer, max_tiles):
    n, d = h1.shape
    tm = EXPERT_TILE
    ff = wg_b.shape[3]
    rows = max_tiles * tm

    def tmap(i, k, te, mt):
        return (jnp.minimum(i, mt[0] - 1), 0)

    def smap(i, k, te, mt):
        return (jnp.minimum(i, mt[0] - 1), 0, 0)

    def wmap(i, k, te, mt):
        return (layer, te[2 * i + k], 0, 0)

    grid_spec = pltpu.PrefetchScalarGridSpec(
        num_scalar_prefetch=2,
        grid=(max_tiles, 2),
        in_specs=[
            pl.BlockSpec((1, 1, tm), smap, memory_space=pltpu.SMEM),
            pl.BlockSpec(memory_space=pl.ANY),
            pl.BlockSpec((tm, LANES), tmap),
            pl.BlockSpec((1, 1, d, ff), wmap),
            pl.BlockSpec((1, 1, d, ff), wmap),
            pl.BlockSpec((1, 1, ff, d), wmap),
        ],
        out_specs=pl.BlockSpec((tm, d), lambda i, k, te, mt: (i, 0)),
        scratch_shapes=[
            pltpu.VMEM((tm, d), jnp.float32),
            pltpu.VMEM((tm, d), jnp.bfloat16),
            pltpu.VMEM((tm, d), jnp.float32),
            pltpu.SemaphoreType.DMA(()),
        ],
    )
    return pl.pallas_call(
        _experts_kernel,
        grid_spec=grid_spec,
        out_shape=jax.ShapeDtypeStruct((rows, d), jnp.float32),
        compiler_params=_cparams(("arbitrary", "arbitrary")),
        name="experts",
    )(tile_e, meta, src3, h1, wab, wg_b, wu_b, wd_b)


def _combine_kernel(pos_ref, ys_hbm, h1_ref, g_ref, b_ref, h_ref, hb_ref, ybuf_ref, sem, *, alpha):
    tn = ybuf_ref.shape[0]
    _gather_rows(pos_ref, ys_hbm, ybuf_ref, sem, tn)
    y = _layer_norm_rows(alpha * h1_ref[...] + ybuf_ref[...], g_ref[0], b_ref[0])
    h_ref[...] = y
    hb_ref[...] = y.astype(jnp.bfloat16)


def _combine(pos3, ys, h1, ln_g, ln_b, layer, alpha):
    n, d = h1.shape
    tn = pos3.shape[2]
    wmap = lambda i: (layer, 0, 0)
    rmap = lambda i: (i, 0)
    return pl.pallas_call(
        functools.partial(_combine_kernel, alpha=alpha),
        grid=(n // tn,),
        in_specs=[
            pl.BlockSpec((1, 1, tn), lambda i: (i, 0, 0), memory_space=pltpu.SMEM),
            pl.BlockSpec(memory_space=pl.ANY),
            pl.BlockSpec((tn, d), rmap),
            _const_spec((1, 1, d), wmap),
            _const_spec((1, 1, d), wmap),
        ],
        out_specs=[pl.BlockSpec((tn, d), rmap), pl.BlockSpec((tn, d), rmap)],
        out_shape=[jax.ShapeDtypeStruct((n, d), jnp.float32), jax.ShapeDtypeStruct((n, d), jnp.bfloat16)],
        scratch_shapes=[pltpu.VMEM((tn, d), jnp.float32), pltpu.SemaphoreType.DMA(())],
        compiler_params=_cparams(("arbitrary",)),
        name="combine",
    )(pos3, ys, h1, ln_g, ln_b)


def _rope_tables(seq, tp):
    rows = seq // GRID_W
    pad = tp - seq - N_META
    pos_row = jnp.concatenate([jnp.repeat(jnp.arange(rows, dtype=jnp.float32), GRID_W),
                               jnp.full((N_META,), -1.0, jnp.float32), jnp.zeros((pad,), jnp.float32)])
    pos_col = jnp.concatenate([jnp.tile(jnp.arange(GRID_W, dtype=jnp.float32), rows),
                               jnp.arange(N_META, dtype=jnp.float32), jnp.zeros((pad,), jnp.float32)])

    def tables(rot_dim):
        n = rot_dim // 4
        inv = ROPE_THETA ** (-jnp.arange(n, dtype=jnp.float32) / n)
        ang = jnp.concatenate([pos_row[:, None] * inv, pos_col[:, None] * inv], axis=-1)
        cos, sin = jnp.cos(ang), jnp.sin(ang)
        reps = LANES // rot_dim
        return (jnp.tile(jnp.concatenate([cos, cos], axis=-1), (1, reps)),
                jnp.tile(jnp.concatenate([-sin, sin], axis=-1), (1, reps)))

    cos_a, sin_a = tables(ROPE_DIM)
    cos_b, sin_b = tables(GQA_HEAD_DIM)
    return cos_a, sin_a, cos_b, sin_b


def _relayout_weights(w_in, w_q_b, w_kv_b):
    bf16 = jnp.bfloat16
    off_kr = Q_LORA + KV_LORA
    off_gq = off_kr + ROPE_DIM
    kr = w_in[:, :, off_kr:off_gq]
    w_in_p = jnp.concatenate([w_in[:, :, :off_kr], kr, kr, w_in[:, :, off_gq:]], axis=-1).astype(bf16)
    depth = w_in.shape[0]
    wq = w_q_b.reshape(depth, Q_LORA, MLA_HEADS, QK_HEAD)
    wq_p = jnp.concatenate([wq[..., :NOPE_DIM].reshape(depth, Q_LORA, -1),
                            wq[..., NOPE_DIM:].reshape(depth, Q_LORA, -1)], axis=-1).astype(bf16)
    wkv = w_kv_b.reshape(depth, KV_LORA, MLA_HEADS, NOPE_DIM + V_DIM)
    wkv_p = jnp.concatenate([wkv[..., :NOPE_DIM].reshape(depth, KV_LORA, -1),
                             wkv[..., NOPE_DIM:].reshape(depth, KV_LORA, -1)], axis=-1).astype(bf16)
    return w_in_p, wq_p, wkv_p


def _dispatch_plan(ri, rw, counts, max_tiles):
    tm = EXPERT_TILE
    n = ri.shape[0]
    rows = max_tiles * tm
    cls, rank = ri[:, 0], ri[:, 1]
    cnt = counts[0, :N_CLASSES].astype(jnp.int32)
    tiles_c = (cnt + tm - 1) // tm
    tile_end = jnp.cumsum(tiles_c)
    tile_start = tile_end - tiles_c
    n_tiles = tile_end[-1]
    routed = cls >= 0
    pos = jnp.where(routed, tile_start[jnp.maximum(cls, 0)] * tm + rank, rows)
    src = jnp.zeros((rows,), jnp.int32).at[pos].set(jnp.arange(n, dtype=jnp.int32), mode="drop")
    wab = jnp.zeros((rows, LANES), jnp.float32).at[pos].set(rw, mode="drop")
    t = jnp.minimum(jnp.arange(max_tiles, dtype=jnp.int32), n_tiles - 1)
    tile_cls = jnp.minimum(jnp.searchsorted(tile_end, t, side="right"), N_CLASSES - 1).astype(jnp.int32)
    pair_lo = jnp.array([0, 0, 0, 1, 1, 2], jnp.int32)
    pair_hi = jnp.array([1, 2, 3, 2, 3, 3], jnp.int32)
    e_a = EXPERTS_PER_GROUP * (tile_cls // PAIRS_PER_GROUP) + pair_lo[tile_cls % PAIRS_PER_GROUP]
    e_b = EXPERTS_PER_GROUP * (tile_cls // PAIRS_PER_GROUP) + pair_hi[tile_cls % PAIRS_PER_GROUP]
    odd = (jnp.arange(max_tiles) % 2) == 1
    first = jnp.where(odd, e_b, e_a)
    second = jnp.where(odd, e_a, e_b)
    tile_e = jnp.stack([first, second], axis=-1).reshape(-1)
    last_e = tile_e[2 * n_tiles - 1]
    step_tile = jnp.arange(2 * max_tiles) // 2
    tile_e = jnp.where(step_tile >= n_tiles, last_e, tile_e).astype(jnp.int32)
    meta = jnp.stack([n_tiles, n_tiles]).astype(jnp.int32)
    pos_tok = jnp.where(routed, pos, 0).astype(jnp.int32)
    return tile_e, meta, src.reshape(max_tiles, 1, tm), wab, pos_tok


def kernel(x, meta_tokens, ln_in_g, ln_in_b, w_in, g_q_lora, w_q_b, g_kv_lora, w_kv_b, g_qk_q, g_qk_k,
           g_out_mla, g_out_gqa, w_out, ln1_g, ln1_b, w_router, router_bias, w_gate, w_up, w_down,
           ln2_g, ln2_b):
    batch, seq, d = x.shape
    depth = w_in.shape[0]
    n_tok = seq + N_META
    tp = -(-n_tok // LANES) * LANES
    n = batch * tp
    alpha = (2.0 * depth) ** 0.25
    bf16 = jnp.bfloat16
    f32 = jnp.float32

    tabs = _rope_tables(seq, tp)
    w_in_p, wq_p, wkv_p = _relayout_weights(w_in, w_q_b, w_kv_b)
    w_out_b = w_out.astype(bf16)
    wg_b, wu_b, wd_b = w_gate.astype(bf16), w_up.astype(bf16), w_down.astype(bf16)
    wr = jnp.pad(w_router.astype(f32), ((0, 0), (0, LANES - N_EXPERTS)))
    wr_hi = wr.astype(bf16)
    wr_lo = (wr - wr_hi.astype(f32)).astype(bf16)
    rbias = jnp.pad(router_bias.astype(f32), (0, LANES - N_EXPERTS)).reshape(1, LANES)
    row3 = lambda a: a.reshape(depth, 1, a.shape[-1])

    meta = jnp.broadcast_to(meta_tokens[None].astype(x.dtype), (batch, N_META, d))
    xin = jnp.concatenate([x, meta, jnp.zeros((batch, tp - n_tok, d), x.dtype)], axis=1).reshape(n, d)
    h, hb = _ln_in(xin, ln_in_g.reshape(1, d), ln_in_b.reshape(1, d))

    max_tiles = -(-(batch * n_tok) // EXPERT_TILE) + N_CLASSES
    tn = _row_tile(n, 256)
    for l in range(depth):
        q_a, k_a, v_a, q_b, k_b, v_b = _in_proj(
            hb, w_in_p, wq_p, wkv_p, row3(g_q_lora), row3(g_kv_lora), row3(g_qk_q), row3(g_qk_k),
            tabs, l, batch, tp)
        o_a = _attention(q_a, k_a, v_a, n_tok)
        o_b = _attention(q_b, k_b, v_b, n_tok)
        h1, ri, rw, counts = _out_proj(
            o_a, o_b, h, w_out_b, row3(g_out_mla), row3(g_out_gqa), row3(ln1_g), row3(ln1_b),
            wr_hi, wr_lo, rbias, l, alpha, tp, n_tok)
        tile_e, tmeta, src3, wab, pos_tok = _dispatch_plan(ri, rw, counts, max_tiles)
        ys = _experts(tile_e, tmeta, src3, h1, wab, wg_b, wu_b, wd_b, l, max_tiles)
        h, hb = _combine(pos_tok.reshape(n // tn, 1, tn), ys, h1, row3(ln2_g), row3(ln2_b), l, alpha)

    return h.reshape(batch, tp, d)[:, :seq]
```

```python
import functools
import math

import jax
import jax.numpy as jnp
from jax import lax
from jax.experimental import pallas as pl
from jax.experimental.pallas import tpu as pltpu

N_META = 16
GRID_W = 64
ROPE_THETA = 10000.0
EPS = 1e-6

MLA_HEADS = 8
Q_LORA = 512
KV_LORA = 256
NOPE_DIM = 128
ROPE_DIM = 64
V_DIM = 128
QK_HEAD = NOPE_DIM + ROPE_DIM
MLA_WIDTH = MLA_HEADS * V_DIM
MLA_SCALE = 1.0 / math.sqrt(QK_HEAD)
MLA_QK_PAD = 256

GQA_HEADS = 8
GQA_KV_HEADS = 2
GQA_HEAD_DIM = 128
GQA_WIDTH = GQA_HEADS * GQA_HEAD_DIM
GQA_SCALE = 1.0 / math.sqrt(GQA_HEAD_DIM)

N_EXPERTS = 16
N_GROUPS = 4
EXPERTS_PER_GROUP = 4
PAIRS_PER_GROUP = 6
N_CLASSES = N_GROUPS * PAIRS_PER_GROUP
EXPERT_FF = 1024

LANES = 128
VMEM_LIMIT_BYTES = 58 * 1024 * 1024
EXPERT_TILE = 256
NEG_BIG = -1e30
HI16 = -65536

C_CQ = 0
C_CKV = C_CQ + Q_LORA
C_KR = C_CKV + KV_LORA
C_GQ = C_KR + LANES
C_GK = C_GQ + GQA_WIDTH
C_GV = C_GK + GQA_KV_HEADS * GQA_HEAD_DIM
IN_COLS_PAD = C_GV + GQA_KV_HEADS * GQA_HEAD_DIM


def _cparams(semantics):
    return pltpu.CompilerParams(dimension_semantics=semantics, vmem_limit_bytes=VMEM_LIMIT_BYTES)


def _const_spec(block_shape, index_map):
    return pl.BlockSpec(block_shape, index_map, pipeline_mode=pl.Buffered(1))


def _row_tile(n, cap, mult=16):
    best = mult
    for t in range(mult, cap + 1, mult):
        if n % t == 0:
            best = t
    assert n % best == 0
    return best


def _layer_norm_rows(z, g, b):
    mu = jnp.mean(z, axis=-1, keepdims=True)
    zc = z - mu
    var = jnp.mean(zc * zc, axis=-1, keepdims=True)
    return zc * lax.rsqrt(var + EPS) * g + b


def _rms_rows(z, g):
    return z * lax.rsqrt(jnp.mean(z * z, axis=-1, keepdims=True) + EPS) * g


def _dot(a, b):
    return jnp.dot(a, b, preferred_element_type=jnp.float32)


def _pack_bf16_pair(lo_f32, hi_f32):
    lo_bits = lax.bitcast_convert_type(lo_f32.astype(jnp.bfloat16).astype(jnp.float32), jnp.int32)
    hi_bits = lax.bitcast_convert_type(hi_f32.astype(jnp.bfloat16).astype(jnp.float32), jnp.int32)
    return lax.shift_right_logical(lo_bits, 16) | (hi_bits & HI16)


def _unpack_bf16_pair(words):
    lo = lax.bitcast_convert_type(words << 16, jnp.float32).astype(jnp.bfloat16)
    hi = lax.bitcast_convert_type(words & HI16, jnp.float32).astype(jnp.bfloat16)
    return lo, hi


def _ln_in_kernel(x_ref, g_ref, b_ref, h_ref, hb_ref):
    y = _layer_norm_rows(x_ref[...], g_ref[...], b_ref[...])
    h_ref[...] = y
    hb_ref[...] = y.astype(jnp.bfloat16)


def _ln_in(xin, g, b):
    n, d = xin.shape
    tm = _row_tile(n, 1024)
    return pl.pallas_call(
        _ln_in_kernel,
        grid=(n // tm,),
        in_specs=[
            pl.BlockSpec((tm, d), lambda i: (i, 0)),
            pl.BlockSpec((1, d), lambda i: (0, 0)),
            pl.BlockSpec((1, d), lambda i: (0, 0)),
        ],
        out_specs=[pl.BlockSpec((tm, d), lambda i: (i, 0)), pl.BlockSpec((tm, d), lambda i: (i, 0))],
        out_shape=[jax.ShapeDtypeStruct((n, d), jnp.float32), jax.ShapeDtypeStruct((n, d), jnp.bfloat16)],
        compiler_params=_cparams(("parallel",)),
        name="ln_in",
    )(xin, g, b)


def _swap_halves_64(x):
    lane = lax.broadcasted_iota(jnp.int32, x.shape, 1)
    fwd = pltpu.roll(x, 32, 1)
    bwd = pltpu.roll(x, 96, 1)
    return jnp.where((lane & 63) < 32, bwd, fwd)


def _in_proj_kernel(x_ref, w_in_ref, wq_ref, wkv_ref, gq_ref, gkv_ref, gqq_ref, gqk_ref,
                    cos_a_ref, sin_a_ref, cos_b_ref, sin_b_ref,
                    q_mla_ref, k_mla_ref, v_mla_ref, q_gqa_ref, k_gqa_ref, v_gqa_ref):
    bf16 = jnp.bfloat16
    u = _dot(x_ref[...], w_in_ref[0])
    cos_a, sin_a = cos_a_ref[...], sin_a_ref[...]
    cos_b, sin_b = cos_b_ref[...], sin_b_ref[...]
    lane = lax.broadcasted_iota(jnp.int32, cos_a.shape, 1)

    c_q = _rms_rows(u[:, C_CQ:C_CKV], gq_ref[0]).astype(bf16)
    qa = _dot(c_q, wq_ref[0])
    c_kv = _rms_rows(u[:, C_CKV:C_KR], gkv_ref[0]).astype(bf16)
    kv = _dot(c_kv, wkv_ref[0])
    kr = u[:, C_KR:C_GQ]
    kr = kr * cos_a + _swap_halves_64(kr) * sin_a
    rope_base = MLA_HEADS * NOPE_DIM
    for pair in range(MLA_HEADS // 2):
        blk = qa[:, rope_base + LANES * pair:rope_base + LANES * (pair + 1)]
        rot = blk * cos_a + _swap_halves_64(blk) * sin_a
        for half in range(2):
            h = 2 * pair + half
            own = jnp.where((lane >> 6) == half, rot, 0.0)
            qh = jnp.concatenate([qa[:, NOPE_DIM * h:NOPE_DIM * (h + 1)], own], axis=1) * MLA_SCALE
            q_mla_ref[0, h] = qh.astype(bf16)
    for h in range(MLA_HEADS):
        kh = jnp.concatenate([kv[:, NOPE_DIM * h:NOPE_DIM * (h + 1)], kr], axis=1)
        k_mla_ref[0, h] = kh.astype(bf16)
        v0 = MLA_HEADS * NOPE_DIM + V_DIM * h
        v_mla_ref[0, h] = kv[:, v0:v0 + V_DIM].astype(bf16)

    for h in range(GQA_HEADS):
        qh = _rms_rows(u[:, C_GQ + GQA_HEAD_DIM * h:C_GQ + GQA_HEAD_DIM * (h + 1)], gqq_ref[0])
        qh = qh * cos_b + pltpu.roll(qh, 64, 1) * sin_b
        q_gqa_ref[0, h] = (qh * GQA_SCALE).astype(bf16)
    for h in range(GQA_KV_HEADS):
        kh = _rms_rows(u[:, C_GK + GQA_HEAD_DIM * h:C_GK + GQA_HEAD_DIM * (h + 1)], gqk_ref[0])
        kh = kh * cos_b + pltpu.roll(kh, 64, 1) * sin_b
        k_gqa_ref[0, h] = kh.astype(bf16)
        v_gqa_ref[0, h] = u[:, C_GV + GQA_HEAD_DIM * h:C_GV + GQA_HEAD_DIM * (h + 1)].astype(bf16)


def _in_proj(hb, w_in_p, wq_p, wkv_p, g_q, g_kv, g_qq, g_qk, tabs, layer, batch, tp):
    n, d = hb.shape
    tm = _row_tile(tp, 320)
    nt = tp // tm
    cos_a, sin_a, cos_b, sin_b = tabs
    wmap = lambda b, i: (layer, 0, 0)
    tmap = lambda b, i: (i, 0)
    omap = lambda b, i: (b, 0, i, 0)
    bf16 = jnp.bfloat16
    return pl.pallas_call(
        _in_proj_kernel,
        grid=(batch, nt),
        in_specs=[
            pl.BlockSpec((tm, d), lambda b, i: (b * nt + i, 0)),
            _const_spec((1, d, IN_COLS_PAD), wmap),
            _const_spec((1, Q_LORA, wq_p.shape[2]), wmap),
            _const_spec((1, KV_LORA, wkv_p.shape[2]), wmap),
            _const_spec((1, 1, Q_LORA), wmap),
            _const_spec((1, 1, KV_LORA), wmap),
            _const_spec((1, 1, GQA_HEAD_DIM), wmap),
            _const_spec((1, 1, GQA_HEAD_DIM), wmap),
            pl.BlockSpec((tm, LANES), tmap),
            pl.BlockSpec((tm, LANES), tmap),
            pl.BlockSpec((tm, LANES), tmap),
            pl.BlockSpec((tm, LANES), tmap),
        ],
        out_specs=[
            pl.BlockSpec((1, MLA_HEADS, tm, MLA_QK_PAD), omap),
            pl.BlockSpec((1, MLA_HEADS, tm, MLA_QK_PAD), omap),
            pl.BlockSpec((1, MLA_HEADS, tm, V_DIM), omap),
            pl.BlockSpec((1, GQA_HEADS, tm, GQA_HEAD_DIM), omap),
            pl.BlockSpec((1, GQA_KV_HEADS, tm, GQA_HEAD_DIM), omap),
            pl.BlockSpec((1, GQA_KV_HEADS, tm, GQA_HEAD_DIM), omap),
        ],
        out_shape=[
            jax.ShapeDtypeStruct((batch, MLA_HEADS, tp, MLA_QK_PAD), bf16),
            jax.ShapeDtypeStruct((batch, MLA_HEADS, tp, MLA_QK_PAD), bf16),
            jax.ShapeDtypeStruct((batch, MLA_HEADS, tp, V_DIM), bf16),
            jax.ShapeDtypeStruct((batch, GQA_HEADS, tp, GQA_HEAD_DIM), bf16),
            jax.ShapeDtypeStruct((batch, GQA_KV_HEADS, tp, GQA_HEAD_DIM), bf16),
            jax.ShapeDtypeStruct((batch, GQA_KV_HEADS, tp, GQA_HEAD_DIM), bf16),
        ],
        compiler_params=_cparams(("parallel", "parallel")),
        name="in_proj",
    )(hb, w_in_p, wq_p, wkv_p, g_q, g_kv, g_qq, g_qk, cos_a, sin_a, cos_b, sin_b)


def _attn_kernel(q_ref, k_ref, v_ref, o_ref, *, tq, n_keys):
    k = k_ref[0, 0]
    v = v_ref[0, 0]
    tp = k.shape[0]
    key_ok = lax.broadcasted_iota(jnp.int32, (1, tp), 1) < n_keys
    for c in range(tp // tq):
        q = q_ref[0, 0, c * tq:(c + 1) * tq, :]
        s = lax.dot_general(q, k, (((1,), (1,)), ((), ())), preferred_element_type=jnp.float32)
        s = jnp.where(key_ok, s, NEG_BIG)
        m = jnp.max(s, axis=-1, keepdims=True)
        p = jnp.exp(s - m)
        l = jnp.sum(p, axis=-1, keepdims=True)
        o = _dot(p.astype(jnp.bfloat16), v)
        o_ref[c * tq:(c + 1) * tq, :] = (o / l).astype(o_ref.dtype)


def _attention(q, k, v, n_keys):
    batch, hq, tp, dk = q.shape
    hk, dv = k.shape[1], v.shape[3]
    rep = hq // hk
    tq = _row_tile(tp, 640)
    return pl.pallas_call(
        functools.partial(_attn_kernel, tq=tq, n_keys=n_keys),
        grid=(batch, hq),
        in_specs=[
            pl.BlockSpec((1, 1, tp, dk), lambda b, h: (b, h, 0, 0)),
            pl.BlockSpec((1, 1, tp, dk), lambda b, h: (b, h // rep, 0, 0)),
            pl.BlockSpec((1, 1, tp, dv), lambda b, h: (b, h // rep, 0, 0)),
        ],
        out_specs=pl.BlockSpec((tp, dv), lambda b, h: (b, h)),
        out_shape=jax.ShapeDtypeStruct((batch * tp, hq * dv), jnp.bfloat16),
        compiler_params=_cparams(("parallel", "parallel")),
        name="attention",
    )(q, k, v)


def _first_index_of_max(vals, lane_f):
    m = jnp.max(vals, axis=-1, keepdims=True)
    idx = jnp.min(jnp.where(vals == m, lane_f, float(LANES)), axis=-1, keepdims=True)
    return m, idx


def _out_proj_kernel(oa_ref, ob_ref, h_ref, w_out_ref, ga_ref, gb_ref, lg_ref, lb_ref,
                     wr_hl_ref, rbias_ref,
                     h1_ref, h1p_ref, ri_ref, cnt_ref, carry_ref, *, alpha, tp, n_tok):
    bf16 = jnp.bfloat16
    f32 = jnp.float32
    i = pl.program_id(0)
    tm, d = h_ref.shape
    half = d // 2

    @pl.when(i == 0)
    def _():
        carry_ref[...] = jnp.zeros_like(carry_ref)

    na = _rms_rows(oa_ref[...].astype(f32), ga_ref[0]).astype(bf16)
    nb = _rms_rows(ob_ref[...].astype(f32), gb_ref[0]).astype(bf16)
    mixed = _dot(na, w_out_ref[0, :MLA_WIDTH, :]) + _dot(nb, w_out_ref[0, MLA_WIDTH:, :])
    h1 = _layer_norm_rows(alpha * h_ref[...] + mixed, lg_ref[0], lb_ref[0])
    h1_ref[...] = h1

    hi = h1.astype(bf16)
    lo = (h1 - hi.astype(f32)).astype(bf16)
    hi_prod = _dot(hi, wr_hl_ref[...])
    logits = hi_prod[:, :LANES] + hi_prod[:, LANES:] + _dot(lo, wr_hl_ref[:, :LANES])
    scores = jax.nn.sigmoid(logits)

    lane = lax.broadcasted_iota(jnp.int32, (tm, LANES), 1)
    lane_f = lane.astype(f32)
    neg = -jnp.inf
    sel = jnp.where(lane < N_EXPERTS, scores + rbias_ref[...], neg)
    grp = lane >> 2

    best = None
    for g in range(N_GROUPS):
        mg = jnp.where(grp == g, sel, neg)
        m1, i1 = _first_index_of_max(mg, lane_f)
        m2 = jnp.max(jnp.where(lane_f == i1, neg, mg), axis=-1, keepdims=True)
        gs = m1 + m2
        if best is None:
            best, gi = gs, jnp.zeros_like(gs)
        else:
            better = gs > best
            gi = jnp.where(better, float(g), gi)
            best = jnp.where(better, gs, best)

    mg = jnp.where(grp.astype(f32) == gi, sel, neg)
    _, e1 = _first_index_of_max(mg, lane_f)
    mg2 = jnp.where(lane_f == e1, neg, mg)
    _, e2 = _first_index_of_max(mg2, lane_f)
    w1 = jnp.sum(jnp.where(lane_f == e1, scores, 0.0), axis=-1, keepdims=True)
    w2 = jnp.sum(jnp.where(lane_f == e2, scores, 0.0), axis=-1, keepdims=True)
    den = w1 + w2
    w1, w2 = w1 / den, w2 / den

    first_lower = e1 < e2
    la = jnp.where(first_lower, e1, e2) - EXPERTS_PER_GROUP * gi
    lb = jnp.where(first_lower, e2, e1) - EXPERTS_PER_GROUP * gi
    w_a = jnp.where(first_lower, w1, w2)
    w_b = jnp.where(first_lower, w2, w1)
    cls = PAIRS_PER_GROUP * gi + la * (7.0 - la) * 0.5 + (lb - la - 1.0)

    row = (i * tm + lax.broadcasted_iota(jnp.int32, (tm, 1), 0)).astype(f32)
    routed = (row - jnp.floor((row + 0.5) * (1.0 / tp)) * tp) < n_tok
    onehot = jnp.where((lane_f == cls) & routed, 1.0, 0.0)

    r_i = lax.broadcasted_iota(jnp.int32, (tm, tm), 0)
    c_i = lax.broadcasted_iota(jnp.int32, (tm, tm), 1)
    lower = jnp.where(c_i < r_i, 1.0, 0.0).astype(bf16)
    before = _dot(lower, onehot.astype(bf16)) + carry_ref[...]
    rank = jnp.sum(onehot * before, axis=-1, keepdims=True)
    carry_ref[...] += jnp.sum(onehot, axis=0, keepdims=True)
    cnt_ref[...] = carry_ref[...]

    cls_out = jnp.where(routed, cls, -1.0)
    ri_ref[...] = jnp.where(lane == 0, cls_out, jnp.where(lane == 1, rank, 0.0)).astype(jnp.int32)

    h1p_ref[:, :half] = _pack_bf16_pair(h1[:, :half], h1[:, half:])
    weights = jnp.where(lane == 0, w_a, jnp.where(lane == 1, w_b, 0.0))
    h1p_ref[:, half:] = lax.bitcast_convert_type(weights, jnp.int32)


def _out_proj(o_a, o_b, h, w_out_b, g_a, g_b, ln_g, ln_b, wr_hl, rbias, layer, alpha, tp, n_tok):
    n, d = h.shape
    tm = _row_tile(n, 256)
    row_w = d // 2 + LANES
    wmap = lambda i: (layer, 0, 0)
    cmap = lambda i: (0, 0)
    rmap = lambda i: (i, 0)
    return pl.pallas_call(
        functools.partial(_out_proj_kernel, alpha=alpha, tp=tp, n_tok=n_tok),
        grid=(n // tm,),
        in_specs=[
            pl.BlockSpec((tm, MLA_WIDTH), rmap),
            pl.BlockSpec((tm, GQA_WIDTH), rmap),
            pl.BlockSpec((tm, d), rmap),
            _const_spec((1, MLA_WIDTH + GQA_WIDTH, d), wmap),
            _const_spec((1, 1, MLA_WIDTH), wmap),
            _const_spec((1, 1, GQA_WIDTH), wmap),
            _const_spec((1, 1, d), wmap),
            _const_spec((1, 1, d), wmap),
            _const_spec((d, 2 * LANES), cmap),
            _const_spec((1, LANES), cmap),
        ],
        out_specs=[
            pl.BlockSpec((tm, d), rmap),
            pl.BlockSpec((tm, row_w), rmap),
            pl.BlockSpec((tm, LANES), rmap),
            pl.BlockSpec((1, LANES), cmap),
        ],
        out_shape=[
            jax.ShapeDtypeStruct((n, d), jnp.float32),
            jax.ShapeDtypeStruct((n, row_w), jnp.int32),
            jax.ShapeDtypeStruct((n, LANES), jnp.int32),
            jax.ShapeDtypeStruct((1, LANES), jnp.float32),
        ],
        scratch_shapes=[pltpu.VMEM((1, LANES), jnp.float32)],
        compiler_params=_cparams(("arbitrary",)),
        name="out_proj",
    )(o_a, o_b, h, w_out_b, g_a, g_b, ln_g, ln_b, wr_hl, rbias)


def _start_row_copies(idx_ref, n_rows, make_copy):
    def issue(r, carry):
        make_copy(r, idx_ref[0, 0, r]).start()
        return carry

    lax.fori_loop(0, n_rows, issue, 0, unroll=8)


def _dispatch_kernel(pos_ref, x_ref, xs_init_hbm, xs_hbm, buf_ref, sem):
    del xs_init_hbm
    i = pl.program_id(0)
    last = pl.num_programs(0) - 1
    slot = i % 2
    tn = x_ref.shape[0]

    def wait_slot(s):
        pltpu.make_async_copy(buf_ref.at[s], xs_hbm.at[pl.ds(0, tn), :], sem.at[s]).wait()

    @pl.when(i >= 2)
    def _():
        wait_slot(slot)

    buf_ref[slot] = x_ref[...]
    _start_row_copies(pos_ref, tn, lambda r, p: pltpu.make_async_copy(
        buf_ref.at[slot, pl.ds(r, 1), :], xs_hbm.at[pl.ds(p, 1), :], sem.at[slot]))

    @pl.when(i == last)
    def _():
        wait_slot(slot)

        @pl.when(last >= 1)
        def _():
            wait_slot(1 - slot)


def _dispatch(pos3, h1p, xs_init):
    n, row_w = h1p.shape
    tn = pos3.shape[2]
    return pl.pallas_call(
        _dispatch_kernel,
        grid=(n // tn,),
        in_specs=[
            pl.BlockSpec((1, 1, tn), lambda i: (i, 0, 0), memory_space=pltpu.SMEM),
            pl.BlockSpec((tn, row_w), lambda i: (i, 0)),
            pl.BlockSpec(memory_space=pl.ANY),
        ],
        out_specs=pl.BlockSpec(memory_space=pl.ANY),
        out_shape=jax.ShapeDtypeStruct(xs_init.shape, xs_init.dtype),
        scratch_shapes=[pltpu.VMEM((2, tn, row_w), h1p.dtype), pltpu.SemaphoreType.DMA((2,))],
        input_output_aliases={2: 0},
        compiler_params=_cparams(("arbitrary",)),
        name="dispatch",
    )(pos3, h1p, xs_init)


def _experts_kernel(tile_e_ref, meta_ref, xs_ref, wg_ref, wu_ref, wd_ref, ys_ref, acc_ref):
    i = pl.program_id(0)
    k = pl.program_id(1)
    half = wg_ref.shape[2] // 2

    @pl.when(i < meta_ref[0])
    def _():
        lo, hi = _unpack_bf16_pair(xs_ref[:, :half])
        g = _dot(lo, wg_ref[0, 0, :half, :]) + _dot(hi, wg_ref[0, 0, half:, :])
        u = _dot(lo, wu_ref[0, 0, :half, :]) + _dot(hi, wu_ref[0, 0, half:, :])
        a = (g * jax.nn.sigmoid(g) * u).astype(jnp.bfloat16)
        y = _dot(a, wd_ref[0, 0])
        second = (k + i) % 2 == 1
        wab = lax.bitcast_convert_type(xs_ref[:, half:], jnp.float32)
        y = y * jnp.where(second, wab[:, 1:2], wab[:, 0:1])

        @pl.when(k == 0)
        def _():
            acc_ref[...] = y

        @pl.when(k == 1)
        def _():
            ys_ref[...] = acc_ref[...] + y

    @pl.when((i >= meta_ref[0]) & (k == 1))
    def _():
        ys_ref[...] = jnp.zeros_like(ys_ref)


def _experts(tile_e, meta, xs, wg_b, wu_b, wd_b, layer, max_tiles):
    tm = EXPERT_TILE
    row_w = xs.shape[1]
    d, ff = wg_b.shape[2], wg_b.shape[3]

    def xmap(i, k, te, mt):
        return (jnp.minimum(i, mt[0] - 1), 0)

    def wmap(i, k, te, mt):
        return (layer, te[2 * i + k], 0, 0)

    grid_spec = pltpu.PrefetchScalarGridSpec(
        num_scalar_prefetch=2,
        grid=(max_tiles, 2),
        in_specs=[
            pl.BlockSpec((tm, row_w), xmap),
            pl.BlockSpec((1, 1, d, ff), wmap),
            pl.BlockSpec((1, 1, d, ff), wmap),
            pl.BlockSpec((1, 1, ff, d), wmap),
        ],
        out_specs=pl.BlockSpec((tm, d), lambda i, k, te, mt: (i, 0)),
        scratch_shapes=[pltpu.VMEM((tm, d), jnp.float32)],
    )
    return pl.pallas_call(
        _experts_kernel,
        grid_spec=grid_spec,
        out_shape=jax.ShapeDtypeStruct((max_tiles * tm, d), jnp.float32),
        compiler_params=_cparams(("arbitrary", "arbitrary")),
        name="experts",
    )(tile_e, meta, xs, wg_b, wu_b, wd_b)


def _combine_kernel(pos_ref, pos_next_ref, ys_hbm, h1_ref, g_ref, b_ref, h_ref, hb_ref, ybuf_ref, sem,
                    *, alpha):
    i = pl.program_id(0)
    last = pl.num_programs(0) - 1
    slot = i % 2
    tn = h1_ref.shape[0]

    def start_gather(idx_ref, s):
        _start_row_copies(idx_ref, tn, lambda r, p: pltpu.make_async_copy(
            ys_hbm.at[pl.ds(p, 1), :], ybuf_ref.at[s, pl.ds(r, 1), :], sem.at[s]))

    @pl.when(i == 0)
    def _():
        start_gather(pos_ref, 0)

    pltpu.make_async_copy(ys_hbm.at[pl.ds(0, tn), :], ybuf_ref.at[slot], sem.at[slot]).wait()

    @pl.when(i < last)
    def _():
        start_gather(pos_next_ref, 1 - slot)

    y = _layer_norm_rows(alpha * h1_ref[...] + ybuf_ref[slot], g_ref[0], b_ref[0])
    h_ref[...] = y
    hb_ref[...] = y.astype(jnp.bfloat16)


def _combine(pos3, ys, h1, ln_g, ln_b, layer, alpha):
    n, d = h1.shape
    tn = pos3.shape[2]
    steps = n // tn
    wmap = lambda i: (layer, 0, 0)
    rmap = lambda i: (i, 0)
    return pl.pallas_call(
        functools.partial(_combine_kernel, alpha=alpha),
        grid=(steps,),
        in_specs=[
            pl.BlockSpec((1, 1, tn), lambda i: (i, 0, 0), memory_space=pltpu.SMEM),
            pl.BlockSpec((1, 1, tn), lambda i: (jnp.minimum(i + 1, steps - 1), 0, 0), memory_space=pltpu.SMEM),
            pl.BlockSpec(memory_space=pl.ANY),
            pl.BlockSpec((tn, d), rmap),
            _const_spec((1, 1, d), wmap),
            _const_spec((1, 1, d), wmap),
        ],
        out_specs=[pl.BlockSpec((tn, d), rmap), pl.BlockSpec((tn, d), rmap)],
        out_shape=[jax.ShapeDtypeStruct((n, d), jnp.float32), jax.ShapeDtypeStruct((n, d), jnp.bfloat16)],
        scratch_shapes=[pltpu.VMEM((2, tn, d), jnp.float32), pltpu.SemaphoreType.DMA((2,))],
        compiler_params=_cparams(("arbitrary",)),
        name="combine",
    )(pos3, pos3, ys, h1, ln_g, ln_b)


def _rope_tables(seq, tp):
    rows = seq // GRID_W
    pad = tp - seq - N_META
    pos_row = jnp.concatenate([jnp.repeat(jnp.arange(rows, dtype=jnp.float32), GRID_W),
                               jnp.full((N_META,), -1.0, jnp.float32), jnp.zeros((pad,), jnp.float32)])
    pos_col = jnp.concatenate([jnp.tile(jnp.arange(GRID_W, dtype=jnp.float32), rows),
                               jnp.arange(N_META, dtype=jnp.float32), jnp.zeros((pad,), jnp.float32)])

    def tables(rot_dim):
        n = rot_dim // 4
        inv = ROPE_THETA ** (-jnp.arange(n, dtype=jnp.float32) / n)
        ang = jnp.concatenate([pos_row[:, None] * inv, pos_col[:, None] * inv], axis=-1)
        cos, sin = jnp.cos(ang), jnp.sin(ang)
        reps = LANES // rot_dim
        return (jnp.tile(jnp.concatenate([cos, cos], axis=-1), (1, reps)),
                jnp.tile(jnp.concatenate([-sin, sin], axis=-1), (1, reps)))

    cos_a, sin_a = tables(ROPE_DIM)
    cos_b, sin_b = tables(GQA_HEAD_DIM)
    return cos_a, sin_a, cos_b, sin_b


def _relayout_weights(w_in, w_q_b, w_kv_b):
    bf16 = jnp.bfloat16
    off_kr = Q_LORA + KV_LORA
    off_gq = off_kr + ROPE_DIM
    kr = w_in[:, :, off_kr:off_gq]
    w_in_p = jnp.concatenate([w_in[:, :, :off_kr], kr, kr, w_in[:, :, off_gq:]], axis=-1).astype(bf16)
    depth = w_in.shape[0]
    wq = w_q_b.reshape(depth, Q_LORA, MLA_HEADS, QK_HEAD)
    wq_p = jnp.concatenate([wq[..., :NOPE_DIM].reshape(depth, Q_LORA, -1),
                            wq[..., NOPE_DIM:].reshape(depth, Q_LORA, -1)], axis=-1).astype(bf16)
    wkv = w_kv_b.reshape(depth, KV_LORA, MLA_HEADS, NOPE_DIM + V_DIM)
    wkv_p = jnp.concatenate([wkv[..., :NOPE_DIM].reshape(depth, KV_LORA, -1),
                             wkv[..., NOPE_DIM:].reshape(depth, KV_LORA, -1)], axis=-1).astype(bf16)
    return w_in_p, wq_p, wkv_p


def _dispatch_plan(ri, counts, max_tiles, tp, n_tok):
    tm = EXPERT_TILE
    n = ri.shape[0]
    rows = max_tiles * tm
    cls, rank = ri[:, 0], ri[:, 1]
    cnt = counts[0, :N_CLASSES].astype(jnp.int32)
    tiles_c = (cnt + tm - 1) // tm
    tile_end = jnp.cumsum(tiles_c)
    tile_start = tile_end - tiles_c
    n_tiles = tile_end[-1]
    routed = cls >= 0
    tok = jnp.arange(n, dtype=jnp.int32)
    spare = rows + (tok // tp) * (tp - n_tok) + (tok % tp - n_tok)
    sorted_pos = tile_start[jnp.maximum(cls, 0)] * tm + rank
    pos_scatter = jnp.where(routed, sorted_pos, spare).astype(jnp.int32)
    pos_gather = jnp.where(routed, sorted_pos, 0).astype(jnp.int32)

    t = jnp.minimum(jnp.arange(max_tiles, dtype=jnp.int32), n_tiles - 1)
    tile_cls = jnp.sum((t[:, None] >= tile_end[None, :]).astype(jnp.int32), axis=1)
    tile_cls = jnp.minimum(tile_cls, N_CLASSES - 1)
    pair_lo = jnp.array([0, 0, 0, 1, 1, 2], jnp.int32)
    pair_hi = jnp.array([1, 2, 3, 2, 3, 3], jnp.int32)
    e_a = EXPERTS_PER_GROUP * (tile_cls // PAIRS_PER_GROUP) + pair_lo[tile_cls % PAIRS_PER_GROUP]
    e_b = EXPERTS_PER_GROUP * (tile_cls // PAIRS_PER_GROUP) + pair_hi[tile_cls % PAIRS_PER_GROUP]
    odd = (jnp.arange(max_tiles) % 2) == 1
    tile_e = jnp.stack([jnp.where(odd, e_b, e_a), jnp.where(odd, e_a, e_b)], axis=-1).reshape(-1)
    step_tile = jnp.arange(2 * max_tiles) // 2
    tile_e = jnp.where(step_tile >= n_tiles, tile_e[2 * n_tiles - 1], tile_e).astype(jnp.int32)
    meta = jnp.stack([n_tiles, n_tiles]).astype(jnp.int32)
    return tile_e, meta, pos_scatter, pos_gather


def kernel(x, meta_tokens, ln_in_g, ln_in_b, w_in, g_q_lora, w_q_b, g_kv_lora, w_kv_b, g_qk_q, g_qk_k,
           g_out_mla, g_out_gqa, w_out, ln1_g, ln1_b, w_router, router_bias, w_gate, w_up, w_down,
           ln2_g, ln2_b):
    batch, seq, d = x.shape
    depth = w_in.shape[0]
    n_tok = seq + N_META
    tp = -(-n_tok // LANES) * LANES
    n = batch * tp
    alpha = (2.0 * depth) ** 0.25
    bf16 = jnp.bfloat16
    f32 = jnp.float32

    tabs = _rope_tables(seq, tp)
    w_in_p, wq_p, wkv_p = _relayout_weights(w_in, w_q_b, w_kv_b)
    w_out_b = w_out.astype(bf16)
    wg_b, wu_b, wd_b = w_gate.astype(bf16), w_up.astype(bf16), w_down.astype(bf16)
    wr = jnp.pad(w_router.astype(f32), ((0, 0), (0, LANES - N_EXPERTS)))
    wr_hi = wr.astype(bf16)
    wr_lo = (wr - wr_hi.astype(f32)).astype(bf16)
    wr_hl = jnp.concatenate([wr_hi, wr_lo], axis=1)
    rbias = jnp.pad(router_bias.astype(f32), (0, LANES - N_EXPERTS)).reshape(1, LANES)
    row3 = lambda a: a.reshape(depth, 1, a.shape[-1])

    meta = jnp.broadcast_to(meta_tokens[None].astype(x.dtype), (batch, N_META, d))
    xin = jnp.concatenate([x, meta, jnp.zeros((batch, tp - n_tok, d), x.dtype)], axis=1).reshape(n, d)
    h, hb = _ln_in(xin, ln_in_g.reshape(1, d), ln_in_b.reshape(1, d))

    tm_e = EXPERT_TILE
    max_tiles = -(-(batch * n_tok) // tm_e) + N_CLASSES
    spare_rows = -(-(batch * (tp - n_tok)) // tm_e) * tm_e
    tn = _row_tile(n, 256)
    for l in range(depth):
        q_a, k_a, v_a, q_b, k_b, v_b = _in_proj(
            hb, w_in_p, wq_p, wkv_p, row3(g_q_lora), row3(g_kv_lora), row3(g_qk_q), row3(g_qk_k),
            tabs, l, batch, tp)
        o_a = _attention(q_a, k_a, v_a, n_tok)
        o_b = _attention(q_b, k_b, v_b, n_tok)
        h1, h1p, ri, counts = _out_proj(
            o_a, o_b, h, w_out_b, row3(g_out_mla), row3(g_out_gqa), row3(ln1_g), row3(ln1_b),
            wr_hl, rbias, l, alpha, tp, n_tok)
        tile_e, tmeta, pos_scatter, pos_gather = _dispatch_plan(ri, counts, max_tiles, tp, n_tok)
        xs_init = jnp.zeros((max_tiles * tm_e + spare_rows, h1p.shape[1]), h1p.dtype)
        xs = _dispatch(pos_scatter.reshape(n // tn, 1, tn), h1p, xs_init)
        ys = _experts(tile_e, tmeta, xs, wg_b, wu_b, wd_b, l, max_tiles)
        h, hb = _combine(pos_gather.reshape(n // tn, 1, tn), ys, h1, row3(ln2_g), row3(ln2_b), l, alpha)

    return h.reshape(batch, tp, d)[:, :seq]
```

```python
import functools
import math

import jax
import jax.numpy as jnp
from jax import lax
from jax.experimental import pallas as pl
from jax.experimental.pallas import tpu as pltpu

N_META = 16
GRID_W = 64
ROPE_THETA = 10000.0
EPS = 1e-6

MLA_HEADS = 8
Q_LORA = 512
KV_LORA = 256
NOPE_DIM = 128
ROPE_DIM = 64
V_DIM = 128
QK_HEAD = NOPE_DIM + ROPE_DIM
MLA_WIDTH = MLA_HEADS * V_DIM
LOG2_E = math.log2(math.e)
MLA_SCALE = LOG2_E / math.sqrt(QK_HEAD)
MLA_QK_PAD = 256

GQA_HEADS = 8
GQA_KV_HEADS = 2
GQA_HEAD_DIM = 128
GQA_WIDTH = GQA_HEADS * GQA_HEAD_DIM
GQA_SCALE = LOG2_E / math.sqrt(GQA_HEAD_DIM)

N_EXPERTS = 16
N_GROUPS = 4
EXPERTS_PER_GROUP = 4
PAIRS_PER_GROUP = 6
N_CLASSES = N_GROUPS * PAIRS_PER_GROUP
EXPERT_FF = 1024

LANES = 128
VMEM_LIMIT_BYTES = 58 * 1024 * 1024
EXPERT_TILE = 256
NEG_BIG = -1e30
HI16 = -65536

C_CQ = 0
C_CKV = C_CQ + Q_LORA
C_KR = C_CKV + KV_LORA
C_GQ = C_KR + LANES
C_GK = C_GQ + GQA_WIDTH
C_GV = C_GK + GQA_KV_HEADS * GQA_HEAD_DIM
IN_COLS_PAD = C_GV + GQA_KV_HEADS * GQA_HEAD_DIM


def _cparams(semantics, flags=None):
    return pltpu.CompilerParams(dimension_semantics=semantics, vmem_limit_bytes=VMEM_LIMIT_BYTES, flags=flags)


def _const_spec(block_shape, index_map):
    return pl.BlockSpec(block_shape, index_map, pipeline_mode=pl.Buffered(1))


def _row_tile(n, cap, mult=16):
    best = mult
    for t in range(mult, cap + 1, mult):
        if n % t == 0:
            best = t
    assert n % best == 0
    return best


def _layer_norm_rows(z, g, b):
    mu = jnp.mean(z, axis=-1, keepdims=True)
    zc = z - mu
    var = jnp.mean(zc * zc, axis=-1, keepdims=True)
    return zc * lax.rsqrt(var + EPS) * g + b


def _rms_rows(z, g):
    return z * lax.rsqrt(jnp.mean(z * z, axis=-1, keepdims=True) + EPS) * g


def _dot(a, b):
    return jnp.dot(a, b, preferred_element_type=jnp.float32)


def _pack_bf16_pair(lo_f32, hi_f32):
    lo_bits = lax.bitcast_convert_type(lo_f32.astype(jnp.bfloat16).astype(jnp.float32), jnp.int32)
    hi_bits = lax.bitcast_convert_type(hi_f32.astype(jnp.bfloat16).astype(jnp.float32), jnp.int32)
    return lax.shift_right_logical(lo_bits, 16) | (hi_bits & HI16)


def _unpack_bf16_pair(words):
    lo = lax.bitcast_convert_type(words << 16, jnp.float32).astype(jnp.bfloat16)
    hi = lax.bitcast_convert_type(words & HI16, jnp.float32).astype(jnp.bfloat16)
    return lo, hi


def _ln_in_kernel(x_ref, meta_ref, g_ref, b_ref, h_ref, hb_ref):
    j = pl.program_id(1)
    last = pl.num_programs(1) - 1

    def emit(rows):
        y = _layer_norm_rows(rows, g_ref[...], b_ref[...])
        h_ref[...] = y
        hb_ref[...] = y.astype(jnp.bfloat16)

    @pl.when(j < last)
    def _():
        emit(x_ref[0])

    @pl.when(j == last)
    def _():
        meta = meta_ref[...]
        pad = jnp.zeros((h_ref.shape[0] - meta.shape[0], meta.shape[1]), meta.dtype)
        emit(jnp.concatenate([meta, pad], axis=0))


def _ln_in(x, meta_tokens, g, b, tp):
    batch, seq, d = x.shape
    tm = tp - seq
    assert seq % tm == 0 and meta_tokens.shape[0] <= tm
    nt = tp // tm
    n = batch * tp
    omap = lambda bi, j: (bi * nt + j, 0)
    cmap = lambda bi, j: (0, 0)
    return pl.pallas_call(
        _ln_in_kernel,
        grid=(batch, nt),
        in_specs=[
            pl.BlockSpec((1, tm, d), lambda bi, j: (bi, jnp.minimum(j, nt - 2), 0)),
            pl.BlockSpec(meta_tokens.shape, cmap),
            pl.BlockSpec((1, d), cmap),
            pl.BlockSpec((1, d), cmap),
        ],
        out_specs=[pl.BlockSpec((tm, d), omap), pl.BlockSpec((tm, d), omap)],
        out_shape=[jax.ShapeDtypeStruct((n, d), jnp.float32), jax.ShapeDtypeStruct((n, d), jnp.bfloat16)],
        compiler_params=_cparams(("parallel", "parallel")),
        name="ln_in",
    )(x, meta_tokens, g, b)


def _swap_halves_64(x):
    lane = lax.broadcasted_iota(jnp.int32, x.shape, 1)
    fwd = pltpu.roll(x, 32, 1)
    bwd = pltpu.roll(x, 96, 1)
    return jnp.where((lane & 63) < 32, bwd, fwd)


def _in_proj_kernel(x_ref, w_in_ref, wq_ref, wkv_ref, gq_ref, gkv_ref, gqq_ref, gqk_ref,
                    cos_a_ref, sin_a_ref, cos_b_ref, sin_b_ref,
                    q_mla_ref, k_mla_ref, v_mla_ref, q_gqa_ref, k_gqa_ref, v_gqa_ref):
    bf16 = jnp.bfloat16
    u = _dot(x_ref[...], w_in_ref[0])
    cos_a, sin_a = cos_a_ref[...], sin_a_ref[...]
    cos_b, sin_b = cos_b_ref[...], sin_b_ref[...]
    lane = lax.broadcasted_iota(jnp.int32, cos_a.shape, 1)

    c_q = _rms_rows(u[:, C_CQ:C_CKV], gq_ref[0]).astype(bf16)
    qa = _dot(c_q, wq_ref[0])
    c_kv = _rms_rows(u[:, C_CKV:C_KR], gkv_ref[0]).astype(bf16)
    kv = _dot(c_kv, wkv_ref[0])
    kr = u[:, C_KR:C_GQ]
    kr = kr * cos_a + _swap_halves_64(kr) * sin_a
    rope_base = MLA_HEADS * NOPE_DIM
    for pair in range(MLA_HEADS // 2):
        blk = qa[:, rope_base + LANES * pair:rope_base + LANES * (pair + 1)]
        rot = blk * cos_a + _swap_halves_64(blk) * sin_a
        for half in range(2):
            h = 2 * pair + half
            own = jnp.where((lane >> 6) == half, rot, 0.0)
            qh = jnp.concatenate([qa[:, NOPE_DIM * h:NOPE_DIM * (h + 1)], own], axis=1) * MLA_SCALE
            q_mla_ref[0, h] = qh.astype(bf16)
    for h in range(MLA_HEADS):
        kh = jnp.concatenate([kv[:, NOPE_DIM * h:NOPE_DIM * (h + 1)], kr], axis=1)
        k_mla_ref[0, h] = kh.astype(bf16)
        v0 = MLA_HEADS * NOPE_DIM + V_DIM * h
        v_mla_ref[0, h] = kv[:, v0:v0 + V_DIM].astype(bf16)

    for h in range(GQA_HEADS):
        qh = _rms_rows(u[:, C_GQ + GQA_HEAD_DIM * h:C_GQ + GQA_HEAD_DIM * (h + 1)], gqq_ref[0])
        qh = qh * cos_b + pltpu.roll(qh, 64, 1) * sin_b
        q_gqa_ref[0, h] = (qh * GQA_SCALE).astype(bf16)
    for h in range(GQA_KV_HEADS):
        kh = _rms_rows(u[:, C_GK + GQA_HEAD_DIM * h:C_GK + GQA_HEAD_DIM * (h + 1)], gqk_ref[0])
        kh = kh * cos_b + pltpu.roll(kh, 64, 1) * sin_b
        k_gqa_ref[0, h] = kh.astype(bf16)
        v_gqa_ref[0, h] = u[:, C_GV + GQA_HEAD_DIM * h:C_GV + GQA_HEAD_DIM * (h + 1)].astype(bf16)


def _in_proj(hb, w_in_p, wq_p, wkv_p, g_q, g_kv, g_qq, g_qk, tabs, layer, batch, tp):
    n, d = hb.shape
    tm = _row_tile(tp, 320)
    nt = tp // tm
    cos_a, sin_a, cos_b, sin_b = tabs
    wmap = lambda b, i: (layer, 0, 0)
    tmap = lambda b, i: (i, 0)
    omap = lambda b, i: (b, 0, i, 0)
    bf16 = jnp.bfloat16
    return pl.pallas_call(
        _in_proj_kernel,
        grid=(batch, nt),
        in_specs=[
            pl.BlockSpec((tm, d), lambda b, i: (b * nt + i, 0)),
            _const_spec((1, d, IN_COLS_PAD), wmap),
            _const_spec((1, Q_LORA, wq_p.shape[2]), wmap),
            _const_spec((1, KV_LORA, wkv_p.shape[2]), wmap),
            _const_spec((1, 1, Q_LORA), wmap),
            _const_spec((1, 1, KV_LORA), wmap),
            _const_spec((1, 1, GQA_HEAD_DIM), wmap),
            _const_spec((1, 1, GQA_HEAD_DIM), wmap),
            pl.BlockSpec((tm, LANES), tmap),
            pl.BlockSpec((tm, LANES), tmap),
            pl.BlockSpec((tm, LANES), tmap),
            pl.BlockSpec((tm, LANES), tmap),
        ],
        out_specs=[
            pl.BlockSpec((1, MLA_HEADS, tm, MLA_QK_PAD), omap),
            pl.BlockSpec((1, MLA_HEADS, tm, MLA_QK_PAD), omap),
            pl.BlockSpec((1, MLA_HEADS, tm, V_DIM), omap),
            pl.BlockSpec((1, GQA_HEADS, tm, GQA_HEAD_DIM), omap),
            pl.BlockSpec((1, GQA_KV_HEADS, tm, GQA_HEAD_DIM), omap),
            pl.BlockSpec((1, GQA_KV_HEADS, tm, GQA_HEAD_DIM), omap),
        ],
        out_shape=[
            jax.ShapeDtypeStruct((batch, MLA_HEADS, tp, MLA_QK_PAD), bf16),
            jax.ShapeDtypeStruct((batch, MLA_HEADS, tp, MLA_QK_PAD), bf16),
            jax.ShapeDtypeStruct((batch, MLA_HEADS, tp, V_DIM), bf16),
            jax.ShapeDtypeStruct((batch, GQA_HEADS, tp, GQA_HEAD_DIM), bf16),
            jax.ShapeDtypeStruct((batch, GQA_KV_HEADS, tp, GQA_HEAD_DIM), bf16),
            jax.ShapeDtypeStruct((batch, GQA_KV_HEADS, tp, GQA_HEAD_DIM), bf16),
        ],
        compiler_params=_cparams(("parallel", "parallel")),
        name="in_proj",
    )(hb, w_in_p, wq_p, wkv_p, g_q, g_kv, g_qq, g_qk, cos_a, sin_a, cos_b, sin_b)


def _attn_kernel(q_ref, k_ref, v_ref, o_ref, *, tq, n_keys):
    tp = k_ref.shape[2]
    body = tp - LANES
    nt = (((1,), (1,)), ((), ()))
    k_body, k_tail = k_ref[0, 0, :body, :], k_ref[0, 0, body:, :]
    v_body, v_tail = v_ref[0, 0, :body, :], v_ref[0, 0, body:, :]
    tail_ok = lax.broadcasted_iota(jnp.int32, (1, LANES), 1) < (n_keys - body)
    n_q = -(-n_keys // 16) * 16
    if n_q < tp:
        o_ref[n_q:, :] = jnp.zeros((tp - n_q, o_ref.shape[1]), o_ref.dtype)
    for r0 in range(0, n_q, tq):
        rows = slice(r0, min(r0 + tq, n_q))
        q = q_ref[0, 0, rows, :]
        s_body = lax.dot_general(q, k_body, nt, preferred_element_type=jnp.float32)
        s_tail = lax.dot_general(q, k_tail, nt, preferred_element_type=jnp.float32)
        s_tail = jnp.where(tail_ok, s_tail, NEG_BIG)
        m = jnp.maximum(jnp.max(s_body, axis=-1, keepdims=True), jnp.max(s_tail, axis=-1, keepdims=True))
        p_body = jnp.exp2(s_body - m)
        p_tail = jnp.exp2(s_tail - m)
        l = jnp.sum(p_body, axis=-1, keepdims=True) + jnp.sum(p_tail, axis=-1, keepdims=True)
        o = _dot(p_body.astype(jnp.bfloat16), v_body) + _dot(p_tail.astype(jnp.bfloat16), v_tail)
        o_ref[rows, :] = (o / l).astype(o_ref.dtype)


def _attention(q, k, v, n_keys):
    batch, hq, tp, dk = q.shape
    hk, dv = k.shape[1], v.shape[3]
    rep = hq // hk
    tq = _row_tile(tp, 320)
    return pl.pallas_call(
        functools.partial(_attn_kernel, tq=tq, n_keys=n_keys),
        grid=(batch, hq),
        in_specs=[
            pl.BlockSpec((1, 1, tp, dk), lambda b, h: (b, h, 0, 0)),
            pl.BlockSpec((1, 1, tp, dk), lambda b, h: (b, h // rep, 0, 0)),
            pl.BlockSpec((1, 1, tp, dv), lambda b, h: (b, h // rep, 0, 0)),
        ],
        out_specs=pl.BlockSpec((tp, dv), lambda b, h: (b, h)),
        out_shape=jax.ShapeDtypeStruct((batch * tp, hq * dv), jnp.bfloat16),
        compiler_params=_cparams(("parallel", "parallel")),
        name="attention",
    )(q, k, v)


def _first_index_of_max(vals, lane_f):
    m = jnp.max(vals, axis=-1, keepdims=True)
    idx = jnp.min(jnp.where(vals == m, lane_f, float(LANES)), axis=-1, keepdims=True)
    return m, idx


def _out_proj_kernel(oa_ref, ob_ref, h_ref, w_out_ref, ga_ref, gb_ref, lg_ref, lb_ref,
                     wr_hl_ref, rbias_ref,
                     h1_ref, h1p_ref, ri_ref, cnt_ref, carry_ref, *, alpha, tp, n_tok, sub):
    i = pl.program_id(0)
    tm = h_ref.shape[0]

    @pl.when(i == 0)
    def _():
        carry_ref[...] = jnp.zeros_like(carry_ref)

    blocks = [slice(r0, r0 + sub) for r0 in range(0, tm, sub)]
    mixed = [_out_proj_matmul(rows, oa_ref, ob_ref, w_out_ref, ga_ref, gb_ref) for rows in blocks]
    routed_in = [_out_proj_norm(rows, m, h_ref, lg_ref, lb_ref, wr_hl_ref, h1_ref, alpha=alpha)
                 for rows, m in zip(blocks, mixed)]
    for rows, (h1, logits) in zip(blocks, routed_in):
        _out_proj_route(rows, i * tm + rows.start, h1, logits, rbias_ref, h1p_ref, ri_ref, carry_ref,
                        tp=tp, n_tok=n_tok)
    cnt_ref[...] = carry_ref[...]


def _out_proj_matmul(rows, oa_ref, ob_ref, w_out_ref, ga_ref, gb_ref):
    bf16 = jnp.bfloat16
    na = _rms_rows(oa_ref[rows, :].astype(jnp.float32), ga_ref[0]).astype(bf16)
    nb = _rms_rows(ob_ref[rows, :].astype(jnp.float32), gb_ref[0]).astype(bf16)
    return _dot(na, w_out_ref[0, :MLA_WIDTH, :]) + _dot(nb, w_out_ref[0, MLA_WIDTH:, :])


def _out_proj_norm(rows, mixed, h_ref, lg_ref, lb_ref, wr_hl_ref, h1_ref, *, alpha):
    bf16 = jnp.bfloat16
    h1 = _layer_norm_rows(alpha * h_ref[rows, :] + mixed, lg_ref[0], lb_ref[0])
    h1_ref[rows, :] = h1
    hi = h1.astype(bf16)
    lo = (h1 - hi.astype(jnp.float32)).astype(bf16)
    hi_prod = _dot(hi, wr_hl_ref[...])
    logits = hi_prod[:, :LANES] + hi_prod[:, LANES:] + _dot(lo, wr_hl_ref[:, :LANES])
    return h1, logits


def _out_proj_route(rows, row0, h1, logits, rbias_ref, h1p_ref, ri_ref, carry_ref, *, tp, n_tok):
    bf16 = jnp.bfloat16
    f32 = jnp.float32
    tm = rows.stop - rows.start
    half = h1.shape[1] // 2
    scores = jax.nn.sigmoid(logits)

    lane = lax.broadcasted_iota(jnp.int32, (tm, LANES), 1)
    lane_f = lane.astype(f32)
    neg = -jnp.inf
    sel = jnp.where(lane < N_EXPERTS, scores + rbias_ref[...], neg)
    grp = lane >> 2

    best = None
    for g in range(N_GROUPS):
        mg = jnp.where(grp == g, sel, neg)
        m1, i1 = _first_index_of_max(mg, lane_f)
        m2 = jnp.max(jnp.where(lane_f == i1, neg, mg), axis=-1, keepdims=True)
        gs = m1 + m2
        if best is None:
            best, gi = gs, jnp.zeros_like(gs)
        else:
            better = gs > best
            gi = jnp.where(better, float(g), gi)
            best = jnp.where(better, gs, best)

    mg = jnp.where(grp.astype(f32) == gi, sel, neg)
    _, e1 = _first_index_of_max(mg, lane_f)
    mg2 = jnp.where(lane_f == e1, neg, mg)
    _, e2 = _first_index_of_max(mg2, lane_f)
    w1 = jnp.sum(jnp.where(lane_f == e1, scores, 0.0), axis=-1, keepdims=True)
    w2 = jnp.sum(jnp.where(lane_f == e2, scores, 0.0), axis=-1, keepdims=True)
    den = w1 + w2
    w1, w2 = w1 / den, w2 / den

    first_lower = e1 < e2
    la = jnp.where(first_lower, e1, e2) - EXPERTS_PER_GROUP * gi
    lb = jnp.where(first_lower, e2, e1) - EXPERTS_PER_GROUP * gi
    w_a = jnp.where(first_lower, w1, w2)
    w_b = jnp.where(first_lower, w2, w1)
    cls = PAIRS_PER_GROUP * gi + la * (7.0 - la) * 0.5 + (lb - la - 1.0)

    row = (row0 + lax.broadcasted_iota(jnp.int32, (tm, 1), 0)).astype(f32)
    routed = (row - jnp.floor((row + 0.5) * (1.0 / tp)) * tp) < n_tok
    onehot = jnp.where((lane_f == cls) & routed, 1.0, 0.0)

    r_i = lax.broadcasted_iota(jnp.int32, (tm, tm), 0)
    c_i = lax.broadcasted_iota(jnp.int32, (tm, tm), 1)
    lower = jnp.where(c_i < r_i, 1.0, 0.0).astype(bf16)
    before = _dot(lower, onehot.astype(bf16)) + carry_ref[...]
    rank = jnp.sum(onehot * before, axis=-1, keepdims=True)
    carry_ref[...] += jnp.sum(onehot, axis=0, keepdims=True)

    cls_out = jnp.where(routed, cls, -1.0)
    ri_ref[rows, :] = jnp.where(lane == 0, cls_out, jnp.where(lane == 1, rank, 0.0)).astype(jnp.int32)

    h1p_ref[rows, :half] = _pack_bf16_pair(h1[:, :half], h1[:, half:])
    weights = jnp.where(lane == 0, w_a, jnp.where(lane == 1, w_b, 0.0))
    h1p_ref[rows, half:] = lax.bitcast_convert_type(weights, jnp.int32)


def _out_proj(o_a, o_b, h, w_out_b, g_a, g_b, ln_g, ln_b, wr_hl, rbias, layer, alpha, tp, n_tok):
    n, d = h.shape
    sub = _row_tile(n, 256)
    tm = 2 * sub if n % (2 * sub) == 0 else sub
    row_w = d // 2 + LANES
    wmap = lambda i: (layer, 0, 0)
    cmap = lambda i: (0, 0)
    rmap = lambda i: (i, 0)
    return pl.pallas_call(
        functools.partial(_out_proj_kernel, alpha=alpha, tp=tp, n_tok=n_tok, sub=sub),
        grid=(n // tm,),
        in_specs=[
            pl.BlockSpec((tm, MLA_WIDTH), rmap),
            pl.BlockSpec((tm, GQA_WIDTH), rmap),
            pl.BlockSpec((tm, d), rmap),
            _const_spec((1, MLA_WIDTH + GQA_WIDTH, d), wmap),
            _const_spec((1, 1, MLA_WIDTH), wmap),
            _const_spec((1, 1, GQA_WIDTH), wmap),
            _const_spec((1, 1, d), wmap),
            _const_spec((1, 1, d), wmap),
            _const_spec((d, 2 * LANES), cmap),
            _const_spec((1, LANES), cmap),
        ],
        out_specs=[
            pl.BlockSpec((tm, d), rmap),
            pl.BlockSpec((tm, row_w), rmap),
            pl.BlockSpec((tm, LANES), rmap),
            pl.BlockSpec((1, LANES), cmap),
        ],
        out_shape=[
            jax.ShapeDtypeStruct((n, d), jnp.float32),
            jax.ShapeDtypeStruct((n, row_w), jnp.int32),
            jax.ShapeDtypeStruct((n, LANES), jnp.int32),
            jax.ShapeDtypeStruct((1, LANES), jnp.float32),
        ],
        scratch_shapes=[pltpu.VMEM((1, LANES), jnp.float32)],
        compiler_params=_cparams(("arbitrary",)),
        name="out_proj",
    )(o_a, o_b, h, w_out_b, g_a, g_b, ln_g, ln_b, wr_hl, rbias)


def _start_row_copies(idx_ref, n_rows, make_copy):
    def issue(r, carry):
        make_copy(r, idx_ref[0, 0, r]).start()
        return carry

    lax.fori_loop(0, n_rows, issue, 0, unroll=8)


def _dispatch_kernel(pos_ref, x_ref, xs_init_hbm, xs_hbm, buf_ref, sem):
    del xs_init_hbm
    i = pl.program_id(0)
    last = pl.num_programs(0) - 1
    slot = i % 2
    tn = x_ref.shape[0]

    def wait_slot(s):
        pltpu.make_async_copy(buf_ref.at[s], xs_hbm.at[pl.ds(0, tn), :], sem.at[s]).wait()

    @pl.when(i >= 2)
    def _():
        wait_slot(slot)

    buf_ref[slot] = x_ref[...]
    _start_row_copies(pos_ref, tn, lambda r, p: pltpu.make_async_copy(
        buf_ref.at[slot, pl.ds(r, 1), :], xs_hbm.at[pl.ds(p, 1), :], sem.at[slot]))

    @pl.when(i == last)
    def _():
        wait_slot(slot)

        @pl.when(last >= 1)
        def _():
            wait_slot(1 - slot)


def _dispatch(pos3, h1p, xs_init):
    n, row_w = h1p.shape
    tn = pos3.shape[2]
    return pl.pallas_call(
        _dispatch_kernel,
        grid=(n // tn,),
        in_specs=[
            pl.BlockSpec((1, 1, tn), lambda i: (i, 0, 0), memory_space=pltpu.SMEM),
            pl.BlockSpec((tn, row_w), lambda i: (i, 0)),
            pl.BlockSpec(memory_space=pl.ANY),
        ],
        out_specs=pl.BlockSpec(memory_space=pl.ANY),
        out_shape=jax.ShapeDtypeStruct(xs_init.shape, xs_init.dtype),
        scratch_shapes=[pltpu.VMEM((2, tn, row_w), h1p.dtype), pltpu.SemaphoreType.DMA((2,))],
        input_output_aliases={2: 0},
        compiler_params=_cparams(("arbitrary",)),
        name="dispatch",
    )(pos3, h1p, xs_init)


def _experts_kernel(tile_e_ref, meta_ref, xs_ref, wg_ref, wu_ref, wd_ref, ys_ref, acc_ref):
    i = pl.program_id(0)
    k = pl.program_id(1)
    half = wg_ref.shape[2] // 2

    @pl.when(i < meta_ref[0])
    def _():
        lo, hi = _unpack_bf16_pair(xs_ref[:, :half])
        g = _dot(lo, wg_ref[0, 0, :half, :]) + _dot(hi, wg_ref[0, 0, half:, :])
        u = _dot(lo, wu_ref[0, 0, :half, :]) + _dot(hi, wu_ref[0, 0, half:, :])
        a = (g * jax.nn.sigmoid(g) * u).astype(jnp.bfloat16)
        y = _dot(a, wd_ref[0, 0])
        second = (k + i) % 2 == 1
        wab = lax.bitcast_convert_type(xs_ref[:, half:], jnp.float32)
        y = y * jnp.where(second, wab[:, 1:2], wab[:, 0:1])

        @pl.when(k == 0)
        def _():
            acc_ref[...] = y

        @pl.when(k == 1)
        def _():
            ys_ref[...] = acc_ref[...] + y

    @pl.when((i >= meta_ref[0]) & (k == 1))
    def _():
        ys_ref[...] = jnp.zeros_like(ys_ref)


def _experts(tile_e, meta, xs, wg_b, wu_b, wd_b, layer, max_tiles):
    tm = EXPERT_TILE
    row_w = xs.shape[1]
    d, ff = wg_b.shape[2], wg_b.shape[3]

    def xmap(i, k, te, mt):
        return (jnp.minimum(i, mt[0] - 1), 0)

    def wmap(i, k, te, mt):
        return (layer, te[2 * i + k], 0, 0)

    grid_spec = pltpu.PrefetchScalarGridSpec(
        num_scalar_prefetch=2,
        grid=(max_tiles, 2),
        in_specs=[
            pl.BlockSpec((tm, row_w), xmap),
            pl.BlockSpec((1, 1, d, ff), wmap),
            pl.BlockSpec((1, 1, d, ff), wmap),
            pl.BlockSpec((1, 1, ff, d), wmap),
        ],
        out_specs=pl.BlockSpec((tm, d), lambda i, k, te, mt: (i, 0)),
        scratch_shapes=[pltpu.VMEM((tm, d), jnp.float32)],
    )
    return pl.pallas_call(
        _experts_kernel,
        grid_spec=grid_spec,
        out_shape=jax.ShapeDtypeStruct((max_tiles * tm, d), jnp.float32),
        compiler_params=_cparams(("arbitrary", "arbitrary")),
        name="experts",
    )(tile_e, meta, xs, wg_b, wu_b, wd_b)


def _combine_kernel(pos_ref, pos_next_ref, ys_hbm, h1_ref, g_ref, b_ref, *rest, alpha, final):
    if final:
        out_ref, ybuf_ref, sem = rest
        i = pl.program_id(0) * pl.num_programs(1) + pl.program_id(1)
        last = pl.num_programs(0) * pl.num_programs(1) - 1
    else:
        h_ref, hb_ref, ybuf_ref, sem = rest
        i = pl.program_id(0)
        last = pl.num_programs(0) - 1
    slot = i % 2
    tn = h1_ref.shape[0]

    def start_gather(idx_ref, s):
        _start_row_copies(idx_ref, tn, lambda r, p: pltpu.make_async_copy(
            ys_hbm.at[pl.ds(p, 1), :], ybuf_ref.at[s, pl.ds(r, 1), :], sem.at[s]))

    @pl.when(i == 0)
    def _():
        start_gather(pos_ref, 0)

    pltpu.make_async_copy(ys_hbm.at[pl.ds(0, tn), :], ybuf_ref.at[slot], sem.at[slot]).wait()

    @pl.when(i < last)
    def _():
        start_gather(pos_next_ref, 1 - slot)

    y = _layer_norm_rows(alpha * h1_ref[...] + ybuf_ref[slot], g_ref[0], b_ref[0])
    if final:
        out_ref[0] = y
    else:
        h_ref[...] = y
        hb_ref[...] = y.astype(jnp.bfloat16)


def _combine_final(pos, ys, h1, ln_g, ln_b, layer, alpha, batch, seq, tp):
    n, d = h1.shape
    tn = tp - seq
    per_batch = tp // tn
    nx = seq // tn
    pos3 = pos.reshape(batch * per_batch, 1, tn)
    wmap = lambda b, j: (layer, 0, 0)
    cur = lambda b, j: b * per_batch + j

    def nxt(b, j):
        wrap = j + 1 >= nx
        return jnp.where(wrap, jnp.minimum(b + 1, batch - 1) * per_batch, b * per_batch + j + 1)

    return pl.pallas_call(
        functools.partial(_combine_kernel, alpha=alpha, final=True),
        grid=(batch, nx),
        in_specs=[
            pl.BlockSpec((1, 1, tn), lambda b, j: (cur(b, j), 0, 0), memory_space=pltpu.SMEM),
            pl.BlockSpec((1, 1, tn), lambda b, j: (nxt(b, j), 0, 0), memory_space=pltpu.SMEM),
            pl.BlockSpec(memory_space=pl.ANY),
            pl.BlockSpec((tn, d), lambda b, j: (cur(b, j), 0)),
            _const_spec((1, 1, d), wmap),
            _const_spec((1, 1, d), wmap),
        ],
        out_specs=pl.BlockSpec((1, tn, d), lambda b, j: (b, j, 0)),
        out_shape=jax.ShapeDtypeStruct((batch, seq, d), jnp.float32),
        scratch_shapes=[pltpu.VMEM((2, tn, d), jnp.float32), pltpu.SemaphoreType.DMA((2,))],
        compiler_params=_cparams(("arbitrary", "arbitrary")),
        name="combine_final",
    )(pos3, pos3, ys, h1, ln_g, ln_b)


def _combine(pos, ys, h1, ln_g, ln_b, layer, alpha, tn):
    n, d = h1.shape
    steps = n // tn
    pos3 = pos.reshape(steps, 1, tn)
    wmap = lambda i: (layer, 0, 0)
    rmap = lambda i: (i, 0)
    return pl.pallas_call(
        functools.partial(_combine_kernel, alpha=alpha, final=False),
        grid=(steps,),
        in_specs=[
            pl.BlockSpec((1, 1, tn), lambda i: (i, 0, 0), memory_space=pltpu.SMEM),
            pl.BlockSpec((1, 1, tn), lambda i: (jnp.minimum(i + 1, steps - 1), 0, 0), memory_space=pltpu.SMEM),
            pl.BlockSpec(memory_space=pl.ANY),
            pl.BlockSpec((tn, d), rmap),
            _const_spec((1, 1, d), wmap),
            _const_spec((1, 1, d), wmap),
        ],
        out_specs=[pl.BlockSpec((tn, d), rmap), pl.BlockSpec((tn, d), rmap)],
        out_shape=[jax.ShapeDtypeStruct((n, d), jnp.float32), jax.ShapeDtypeStruct((n, d), jnp.bfloat16)],
        scratch_shapes=[pltpu.VMEM((2, tn, d), jnp.float32), pltpu.SemaphoreType.DMA((2,))],
        compiler_params=_cparams(("arbitrary",)),
        name="combine",
    )(pos3, pos3, ys, h1, ln_g, ln_b)


def _rope_tables(seq, tp):
    rows = seq // GRID_W
    pad = tp - seq - N_META
    pos_row = jnp.concatenate([jnp.repeat(jnp.arange(rows, dtype=jnp.float32), GRID_W),
                               jnp.full((N_META,), -1.0, jnp.float32), jnp.zeros((pad,), jnp.float32)])
    pos_col = jnp.concatenate([jnp.tile(jnp.arange(GRID_W, dtype=jnp.float32), rows),
                               jnp.arange(N_META, dtype=jnp.float32), jnp.zeros((pad,), jnp.float32)])

    def tables(rot_dim):
        n = rot_dim // 4
        inv = ROPE_THETA ** (-jnp.arange(n, dtype=jnp.float32) / n)
        ang = jnp.concatenate([pos_row[:, None] * inv, pos_col[:, None] * inv], axis=-1)
        cos, sin = jnp.cos(ang), jnp.sin(ang)
        reps = LANES // rot_dim
        return (jnp.tile(jnp.concatenate([cos, cos], axis=-1), (1, reps)),
                jnp.tile(jnp.concatenate([-sin, sin], axis=-1), (1, reps)))

    cos_a, sin_a = tables(ROPE_DIM)
    cos_b, sin_b = tables(GQA_HEAD_DIM)
    return cos_a, sin_a, cos_b, sin_b


def _relayout_weights(w_in, w_q_b, w_kv_b):
    bf16 = jnp.bfloat16
    off_kr = Q_LORA + KV_LORA
    off_gq = off_kr + ROPE_DIM
    kr = w_in[:, :, off_kr:off_gq]
    w_in_p = jnp.concatenate([w_in[:, :, :off_kr], kr, kr, w_in[:, :, off_gq:]], axis=-1).astype(bf16)
    depth = w_in.shape[0]
    wq = w_q_b.reshape(depth, Q_LORA, MLA_HEADS, QK_HEAD)
    wq_p = jnp.concatenate([wq[..., :NOPE_DIM].reshape(depth, Q_LORA, -1),
                            wq[..., NOPE_DIM:].reshape(depth, Q_LORA, -1)], axis=-1).astype(bf16)
    wkv = w_kv_b.reshape(depth, KV_LORA, MLA_HEADS, NOPE_DIM + V_DIM)
    wkv_p = jnp.concatenate([wkv[..., :NOPE_DIM].reshape(depth, KV_LORA, -1),
                             wkv[..., NOPE_DIM:].reshape(depth, KV_LORA, -1)], axis=-1).astype(bf16)
    return w_in_p, wq_p, wkv_p


def _dispatch_plan(ri, counts, max_tiles, tp, n_tok):
    tm = EXPERT_TILE
    n = ri.shape[0]
    rows = max_tiles * tm
    cls, rank = ri[:, 0], ri[:, 1]
    cnt = counts[0, :N_CLASSES].astype(jnp.int32)
    tiles_c = (cnt + tm - 1) // tm
    tile_end = jnp.cumsum(tiles_c)
    tile_start = tile_end - tiles_c
    n_tiles = tile_end[-1]
    routed = cls >= 0
    tok = jnp.arange(n, dtype=jnp.int32)
    spare = rows + (tok // tp) * (tp - n_tok) + (tok % tp - n_tok)
    sorted_pos = tile_start[jnp.maximum(cls, 0)] * tm + rank
    pos_scatter = jnp.where(routed, sorted_pos, spare).astype(jnp.int32)
    pos_gather = jnp.where(routed, sorted_pos, 0).astype(jnp.int32)

    t = jnp.minimum(jnp.arange(max_tiles, dtype=jnp.int32), n_tiles - 1)
    tile_cls = jnp.sum((t[:, None] >= tile_end[None, :]).astype(jnp.int32), axis=1)
    tile_cls = jnp.minimum(tile_cls, N_CLASSES - 1)
    pair_lo = jnp.array([0, 0, 0, 1, 1, 2], jnp.int32)
    pair_hi = jnp.array([1, 2, 3, 2, 3, 3], jnp.int32)
    e_a = EXPERTS_PER_GROUP * (tile_cls // PAIRS_PER_GROUP) + pair_lo[tile_cls % PAIRS_PER_GROUP]
    e_b = EXPERTS_PER_GROUP * (tile_cls // PAIRS_PER_GROUP) + pair_hi[tile_cls % PAIRS_PER_GROUP]
    odd = (jnp.arange(max_tiles) % 2) == 1
    tile_e = jnp.stack([jnp.where(odd, e_b, e_a), jnp.where(odd, e_a, e_b)], axis=-1).reshape(-1)
    step_tile = jnp.arange(2 * max_tiles) // 2
    tile_e = jnp.where(step_tile >= n_tiles, tile_e[2 * n_tiles - 1], tile_e).astype(jnp.int32)
    meta = jnp.stack([n_tiles, n_tiles]).astype(jnp.int32)
    return tile_e, meta, pos_scatter, pos_gather


def kernel(x, meta_tokens, ln_in_g, ln_in_b, w_in, g_q_lora, w_q_b, g_kv_lora, w_kv_b, g_qk_q, g_qk_k,
           g_out_mla, g_out_gqa, w_out, ln1_g, ln1_b, w_router, router_bias, w_gate, w_up, w_down,
           ln2_g, ln2_b):
    batch, seq, d = x.shape
    depth = w_in.shape[0]
    n_tok = seq + N_META
    tp = -(-n_tok // LANES) * LANES
    n = batch * tp
    alpha = (2.0 * depth) ** 0.25
    bf16 = jnp.bfloat16
    f32 = jnp.float32

    tabs = _rope_tables(seq, tp)
    w_in_p, wq_p, wkv_p = _relayout_weights(w_in, w_q_b, w_kv_b)
    w_out_b = w_out.astype(bf16)
    wg_b, wu_b, wd_b = w_gate.astype(bf16), w_up.astype(bf16), w_down.astype(bf16)
    wr = jnp.pad(w_router.astype(f32), ((0, 0), (0, LANES - N_EXPERTS)))
    wr_hi = wr.astype(bf16)
    wr_lo = (wr - wr_hi.astype(f32)).astype(bf16)
    wr_hl = jnp.concatenate([wr_hi, wr_lo], axis=1)
    rbias = jnp.pad(router_bias.astype(f32), (0, LANES - N_EXPERTS)).reshape(1, LANES)
    row3 = lambda a: a.reshape(depth, 1, a.shape[-1])

    h, hb = _ln_in(x, meta_tokens.astype(x.dtype), ln_in_g.reshape(1, d), ln_in_b.reshape(1, d), tp)

    tm_e = EXPERT_TILE
    max_tiles = -(-(batch * n_tok) // tm_e) + N_CLASSES
    spare_rows = -(-(batch * (tp - n_tok)) // tm_e) * tm_e
    tn = _row_tile(n, 256)
    xs = jnp.zeros((max_tiles * tm_e + spare_rows, d // 2 + LANES), jnp.int32)
    for l in range(depth):
        q_a, k_a, v_a, q_b, k_b, v_b = _in_proj(
            hb, w_in_p, wq_p, wkv_p, row3(g_q_lora), row3(g_kv_lora), row3(g_qk_q), row3(g_qk_k),
            tabs, l, batch, tp)
        o_a = _attention(q_a, k_a, v_a, n_tok)
        o_b = _attention(q_b, k_b, v_b, n_tok)
        h1, h1p, ri, counts = _out_proj(
            o_a, o_b, h, w_out_b, row3(g_out_mla), row3(g_out_gqa), row3(ln1_g), row3(ln1_b),
            wr_hl, rbias, l, alpha, tp, n_tok)
        tile_e, tmeta, pos_scatter, pos_gather = _dispatch_plan(ri, counts, max_tiles, tp, n_tok)
        xs = _dispatch(pos_scatter.reshape(n // tn, 1, tn), h1p, xs)
        ys = _experts(tile_e, tmeta, xs, wg_b, wu_b, wd_b, l, max_tiles)
        if l + 1 < depth:
            h, hb = _combine(pos_gather, ys, h1, row3(ln2_g), row3(ln2_b), l, alpha, tn)
        else:
            out = _combine_final(pos_gather, ys, h1, row3(ln2_g), row3(ln2_b), l, alpha, batch, seq, tp)
    return out
```

```python
import functools
import math

import jax
import jax.numpy as jnp
from jax import lax
from jax.experimental import pallas as pl
from jax.experimental.pallas import tpu as pltpu

N_META = 16
GRID_W = 64
ROPE_THETA = 10000.0
EPS = 1e-6

MLA_HEADS = 8
Q_LORA = 512
KV_LORA = 256
NOPE_DIM = 128
ROPE_DIM = 64
V_DIM = 128
QK_HEAD = NOPE_DIM + ROPE_DIM
MLA_WIDTH = MLA_HEADS * V_DIM
LOG2_E = math.log2(math.e)
MLA_SCALE = LOG2_E / math.sqrt(QK_HEAD)
MLA_QK_PAD = 256

GQA_HEADS = 8
GQA_KV_HEADS = 2
GQA_HEAD_DIM = 128
GQA_WIDTH = GQA_HEADS * GQA_HEAD_DIM
GQA_SCALE = LOG2_E / math.sqrt(GQA_HEAD_DIM)

N_EXPERTS = 16
N_GROUPS = 4
EXPERTS_PER_GROUP = 4
PAIRS_PER_GROUP = 6
N_CLASSES = N_GROUPS * PAIRS_PER_GROUP
EXPERT_FF = 1024

LANES = 128
VMEM_LIMIT_BYTES = 58 * 1024 * 1024
EXPERT_TILE = 256
NEG_BIG = -1e30
HI16 = -65536

C_CQ = 0
C_CKV = C_CQ + Q_LORA
C_KR = C_CKV + KV_LORA
C_GQ = C_KR + LANES
C_GK = C_GQ + GQA_WIDTH
C_GV = C_GK + GQA_KV_HEADS * GQA_HEAD_DIM
IN_COLS_PAD = C_GV + GQA_KV_HEADS * GQA_HEAD_DIM


def _cparams(semantics, flags=None):
    return pltpu.CompilerParams(dimension_semantics=semantics, vmem_limit_bytes=VMEM_LIMIT_BYTES, flags=flags)


def _const_spec(block_shape, index_map):
    return pl.BlockSpec(block_shape, index_map, pipeline_mode=pl.Buffered(1))


def _row_tile(n, cap, mult=16):
    best = mult
    for t in range(mult, cap + 1, mult):
        if n % t == 0:
            best = t
    assert n % best == 0
    return best


def _layer_norm_rows(z, g, b):
    mu = jnp.mean(z, axis=-1, keepdims=True)
    zc = z - mu
    var = jnp.mean(zc * zc, axis=-1, keepdims=True)
    return zc * lax.rsqrt(var + EPS) * g + b


def _rms_rows(z, g):
    return z * lax.rsqrt(jnp.mean(z * z, axis=-1, keepdims=True) + EPS) * g


def _dot(a, b):
    return jnp.dot(a, b, preferred_element_type=jnp.float32)


def _pack_bf16_pair(lo_f32, hi_f32):
    lo_bits = lax.bitcast_convert_type(lo_f32.astype(jnp.bfloat16).astype(jnp.float32), jnp.int32)
    hi_bits = lax.bitcast_convert_type(hi_f32.astype(jnp.bfloat16).astype(jnp.float32), jnp.int32)
    return lax.shift_right_logical(lo_bits, 16) | (hi_bits & HI16)


def _unpack_bf16_pair(words):
    lo = lax.bitcast_convert_type(words << 16, jnp.float32).astype(jnp.bfloat16)
    hi = lax.bitcast_convert_type(words & HI16, jnp.float32).astype(jnp.bfloat16)
    return lo, hi


def _ln_in_kernel(x_ref, meta_ref, g_ref, b_ref, h_ref, hb_ref):
    j = pl.program_id(1)
    last = pl.num_programs(1) - 1

    def emit(rows):
        y = _layer_norm_rows(rows, g_ref[...], b_ref[...])
        h_ref[...] = y
        hb_ref[...] = y.astype(jnp.bfloat16)

    @pl.when(j < last)
    def _():
        emit(x_ref[0])

    @pl.when(j == last)
    def _():
        meta = meta_ref[...]
        pad = jnp.zeros((h_ref.shape[0] - meta.shape[0], meta.shape[1]), meta.dtype)
        emit(jnp.concatenate([meta, pad], axis=0))


def _ln_in(x, meta_tokens, g, b, tp):
    batch, seq, d = x.shape
    tm = tp - seq
    assert seq % tm == 0 and meta_tokens.shape[0] <= tm
    nt = tp // tm
    n = batch * tp
    omap = lambda bi, j: (bi * nt + j, 0)
    cmap = lambda bi, j: (0, 0)
    return pl.pallas_call(
        _ln_in_kernel,
        grid=(batch, nt),
        in_specs=[
            pl.BlockSpec((1, tm, d), lambda bi, j: (bi, jnp.minimum(j, nt - 2), 0)),
            pl.BlockSpec(meta_tokens.shape, cmap),
            pl.BlockSpec((1, d), cmap),
            pl.BlockSpec((1, d), cmap),
        ],
        out_specs=[pl.BlockSpec((tm, d), omap), pl.BlockSpec((tm, d), omap)],
        out_shape=[jax.ShapeDtypeStruct((n, d), jnp.float32), jax.ShapeDtypeStruct((n, d), jnp.bfloat16)],
        compiler_params=_cparams(("parallel", "parallel")),
        name="ln_in",
    )(x, meta_tokens, g, b)


def _swap_halves_64(x):
    lane = lax.broadcasted_iota(jnp.int32, x.shape, 1)
    fwd = pltpu.roll(x, 32, 1)
    bwd = pltpu.roll(x, 96, 1)
    return jnp.where((lane & 63) < 32, bwd, fwd)


def _in_proj_kernel(x_ref, w_in_ref, wq_ref, wkv_ref, gq_ref, gkv_ref, gqq_ref, gqk_ref,
                    cos_a_ref, sin_a_ref, cos_b_ref, sin_b_ref,
                    q_mla_ref, k_mla_ref, v_mla_ref, q_gqa_ref, k_gqa_ref, v_gqa_ref):
    bf16 = jnp.bfloat16
    u = _dot(x_ref[...], w_in_ref[0])
    cos_a, sin_a = cos_a_ref[...], sin_a_ref[...]
    cos_b, sin_b = cos_b_ref[...], sin_b_ref[...]
    lane = lax.broadcasted_iota(jnp.int32, cos_a.shape, 1)

    c_q = _rms_rows(u[:, C_CQ:C_CKV], gq_ref[0]).astype(bf16)
    qa = _dot(c_q, wq_ref[0])
    c_kv = _rms_rows(u[:, C_CKV:C_KR], gkv_ref[0]).astype(bf16)
    kv = _dot(c_kv, wkv_ref[0])
    kr = u[:, C_KR:C_GQ]
    kr = kr * cos_a + _swap_halves_64(kr) * sin_a
    rope_base = MLA_HEADS * NOPE_DIM
    for pair in range(MLA_HEADS // 2):
        blk = qa[:, rope_base + LANES * pair:rope_base + LANES * (pair + 1)]
        rot = blk * cos_a + _swap_halves_64(blk) * sin_a
        for half in range(2):
            h = 2 * pair + half
            own = jnp.where((lane >> 6) == half, rot, 0.0)
            qh = jnp.concatenate([qa[:, NOPE_DIM * h:NOPE_DIM * (h + 1)], own], axis=1) * MLA_SCALE
            q_mla_ref[0, h] = qh.astype(bf16)
    for h in range(MLA_HEADS):
        kh = jnp.concatenate([kv[:, NOPE_DIM * h:NOPE_DIM * (h + 1)], kr], axis=1)
        k_mla_ref[0, h] = kh.astype(bf16)
        v0 = MLA_HEADS * NOPE_DIM + V_DIM * h
        v_mla_ref[0, h] = kv[:, v0:v0 + V_DIM].astype(bf16)

    for h in range(GQA_HEADS):
        qh = _rms_rows(u[:, C_GQ + GQA_HEAD_DIM * h:C_GQ + GQA_HEAD_DIM * (h + 1)], gqq_ref[0])
        qh = qh * cos_b + pltpu.roll(qh, 64, 1) * sin_b
        q_gqa_ref[0, h] = (qh * GQA_SCALE).astype(bf16)
    for h in range(GQA_KV_HEADS):
        kh = _rms_rows(u[:, C_GK + GQA_HEAD_DIM * h:C_GK + GQA_HEAD_DIM * (h + 1)], gqk_ref[0])
        kh = kh * cos_b + pltpu.roll(kh, 64, 1) * sin_b
        k_gqa_ref[0, h] = kh.astype(bf16)
        v_gqa_ref[0, h] = u[:, C_GV + GQA_HEAD_DIM * h:C_GV + GQA_HEAD_DIM * (h + 1)].astype(bf16)


def _in_proj(hb, w_in_p, wq_p, wkv_p, g_q, g_kv, g_qq, g_qk, tabs, layer, batch, tp):
    n, d = hb.shape
    tm = _row_tile(tp, 320)
    nt = tp // tm
    cos_a, sin_a, cos_b, sin_b = tabs
    wmap = lambda b, i: (layer, 0, 0)
    tmap = lambda b, i: (i, 0)
    omap = lambda b, i: (b, 0, i, 0)
    bf16 = jnp.bfloat16
    return pl.pallas_call(
        _in_proj_kernel,
        grid=(batch, nt),
        in_specs=[
            pl.BlockSpec((tm, d), lambda b, i: (b * nt + i, 0)),
            _const_spec((1, d, IN_COLS_PAD), wmap),
            _const_spec((1, Q_LORA, wq_p.shape[2]), wmap),
            _const_spec((1, KV_LORA, wkv_p.shape[2]), wmap),
            _const_spec((1, 1, Q_LORA), wmap),
            _const_spec((1, 1, KV_LORA), wmap),
            _const_spec((1, 1, GQA_HEAD_DIM), wmap),
            _const_spec((1, 1, GQA_HEAD_DIM), wmap),
            pl.BlockSpec((tm, LANES), tmap),
            pl.BlockSpec((tm, LANES), tmap),
            pl.BlockSpec((tm, LANES), tmap),
            pl.BlockSpec((tm, LANES), tmap),
        ],
        out_specs=[
            pl.BlockSpec((1, MLA_HEADS, tm, MLA_QK_PAD), omap),
            pl.BlockSpec((1, MLA_HEADS, tm, MLA_QK_PAD), omap),
            pl.BlockSpec((1, MLA_HEADS, tm, V_DIM), omap),
            pl.BlockSpec((1, GQA_HEADS, tm, GQA_HEAD_DIM), omap),
            pl.BlockSpec((1, GQA_KV_HEADS, tm, GQA_HEAD_DIM), omap),
            pl.BlockSpec((1, GQA_KV_HEADS, tm, GQA_HEAD_DIM), omap),
        ],
        out_shape=[
            jax.ShapeDtypeStruct((batch, MLA_HEADS, tp, MLA_QK_PAD), bf16),
            jax.ShapeDtypeStruct((batch, MLA_HEADS, tp, MLA_QK_PAD), bf16),
            jax.ShapeDtypeStruct((batch, MLA_HEADS, tp, V_DIM), bf16),
            jax.ShapeDtypeStruct((batch, GQA_HEADS, tp, GQA_HEAD_DIM), bf16),
            jax.ShapeDtypeStruct((batch, GQA_KV_HEADS, tp, GQA_HEAD_DIM), bf16),
            jax.ShapeDtypeStruct((batch, GQA_KV_HEADS, tp, GQA_HEAD_DIM), bf16),
        ],
        compiler_params=_cparams(("parallel", "parallel")),
        name="in_proj",
    )(hb, w_in_p, wq_p, wkv_p, g_q, g_kv, g_qq, g_qk, cos_a, sin_a, cos_b, sin_b)


def _attn_kernel(q_ref, k_ref, v_ref, *rest, tq, n_keys, n_cast):
    o_ref = rest[n_cast]
    for src_ref, dst_ref in zip(rest[:n_cast], rest[n_cast + 1:]):
        dst_ref[...] = src_ref[...].astype(dst_ref.dtype)
    tp = k_ref.shape[2]
    body = tp - LANES
    nt = (((1,), (1,)), ((), ()))
    k_body, k_tail = k_ref[0, 0, :body, :], k_ref[0, 0, body:, :]
    v_body, v_tail = v_ref[0, 0, :body, :], v_ref[0, 0, body:, :]
    tail_ok = lax.broadcasted_iota(jnp.int32, (1, LANES), 1) < (n_keys - body)
    n_q = -(-n_keys // 16) * 16
    if n_q < tp:
        o_ref[n_q:, :] = jnp.zeros((tp - n_q, o_ref.shape[1]), o_ref.dtype)
    for r0 in range(0, n_q, tq):
        rows = slice(r0, min(r0 + tq, n_q))
        q = q_ref[0, 0, rows, :]
        s_body = lax.dot_general(q, k_body, nt, preferred_element_type=jnp.float32)
        s_tail = lax.dot_general(q, k_tail, nt, preferred_element_type=jnp.float32)
        s_tail = jnp.where(tail_ok, s_tail, NEG_BIG)
        m = jnp.maximum(jnp.max(s_body, axis=-1, keepdims=True), jnp.max(s_tail, axis=-1, keepdims=True))
        p_body = jnp.exp2(s_body - m)
        p_tail = jnp.exp2(s_tail - m)
        l = jnp.sum(p_body, axis=-1, keepdims=True) + jnp.sum(p_tail, axis=-1, keepdims=True)
        o = _dot(p_body.astype(jnp.bfloat16), v_body) + _dot(p_tail.astype(jnp.bfloat16), v_tail)
        o_ref[rows, :] = (o / l).astype(o_ref.dtype)


def _attention(q, k, v, n_keys, casts):
    batch, hq, tp, dk = q.shape
    hk, dv = k.shape[1], v.shape[3]
    rep = hq // hk
    tq = _row_tile(tp, 320)
    steps = batch * hq
    cast_in, cast_out, cast_shapes = [], [], []
    for w, layer in casts:
        rows = w.shape[0] // (layer[1])
        assert rows % steps == 0
        blk = rows // steps
        off = layer[0] * steps
        cast_in.append(pl.BlockSpec((blk, w.shape[1]), lambda b, h, off=off: (off + b * hq + h, 0)))
        cast_out.append(pl.BlockSpec((blk, w.shape[1]), lambda b, h: (b * hq + h, 0)))
        cast_shapes.append(jax.ShapeDtypeStruct((rows, w.shape[1]), jnp.bfloat16))
    outs = pl.pallas_call(
        functools.partial(_attn_kernel, tq=tq, n_keys=n_keys, n_cast=len(casts)),
        grid=(batch, hq),
        in_specs=[
            pl.BlockSpec((1, 1, tp, dk), lambda b, h: (b, h, 0, 0)),
            pl.BlockSpec((1, 1, tp, dk), lambda b, h: (b, h // rep, 0, 0)),
            pl.BlockSpec((1, 1, tp, dv), lambda b, h: (b, h // rep, 0, 0)),
        ] + cast_in,
        out_specs=[pl.BlockSpec((tp, dv), lambda b, h: (b, h))] + cast_out,
        out_shape=[jax.ShapeDtypeStruct((batch * tp, hq * dv), jnp.bfloat16)] + cast_shapes,
        compiler_params=_cparams(("parallel", "parallel")),
        name="attention",
    )(q, k, v, *[w for w, _ in casts])
    return outs[0], outs[1:]


def _first_index_of_max(vals, lane_f):
    m = jnp.max(vals, axis=-1, keepdims=True)
    idx = jnp.min(jnp.where(vals == m, lane_f, float(LANES)), axis=-1, keepdims=True)
    return m, idx


def _out_proj_kernel(oa_ref, ob_ref, h_ref, w_out_ref, ga_ref, gb_ref, lg_ref, lb_ref,
                     wr_hl_ref, rbias_ref,
                     h1_ref, h1p_ref, ri_ref, cnt_ref, carry_ref, *, alpha, tp, n_tok, sub):
    i = pl.program_id(0)
    tm = h_ref.shape[0]

    @pl.when(i == 0)
    def _():
        carry_ref[...] = jnp.zeros_like(carry_ref)

    blocks = [slice(r0, r0 + sub) for r0 in range(0, tm, sub)]
    mixed = [_out_proj_matmul(rows, oa_ref, ob_ref, w_out_ref, ga_ref, gb_ref) for rows in blocks]
    routed_in = [_out_proj_norm(rows, m, h_ref, lg_ref, lb_ref, wr_hl_ref, h1_ref, alpha=alpha)
                 for rows, m in zip(blocks, mixed)]
    for rows, (h1, logits) in zip(blocks, routed_in):
        _out_proj_route(rows, i * tm + rows.start, h1, logits, rbias_ref, h1p_ref, ri_ref, carry_ref,
                        tp=tp, n_tok=n_tok)
    cnt_ref[...] = carry_ref[...]


def _out_proj_matmul(rows, oa_ref, ob_ref, w_out_ref, ga_ref, gb_ref):
    bf16 = jnp.bfloat16
    na = _rms_rows(oa_ref[rows, :].astype(jnp.float32), ga_ref[0]).astype(bf16)
    nb = _rms_rows(ob_ref[rows, :].astype(jnp.float32), gb_ref[0]).astype(bf16)
    return _dot(na, w_out_ref[0, :MLA_WIDTH, :]) + _dot(nb, w_out_ref[0, MLA_WIDTH:, :])


def _out_proj_norm(rows, mixed, h_ref, lg_ref, lb_ref, wr_hl_ref, h1_ref, *, alpha):
    bf16 = jnp.bfloat16
    h1 = _layer_norm_rows(alpha * h_ref[rows, :] + mixed, lg_ref[0], lb_ref[0])
    h1_ref[rows, :] = h1
    hi = h1.astype(bf16)
    lo = (h1 - hi.astype(jnp.float32)).astype(bf16)
    hi_prod = _dot(hi, wr_hl_ref[...])
    logits = hi_prod[:, :LANES] + hi_prod[:, LANES:] + _dot(lo, wr_hl_ref[:, :LANES])
    return h1, logits


def _out_proj_route(rows, row0, h1, logits, rbias_ref, h1p_ref, ri_ref, carry_ref, *, tp, n_tok):
    bf16 = jnp.bfloat16
    f32 = jnp.float32
    tm = rows.stop - rows.start
    half = h1.shape[1] // 2
    scores = jax.nn.sigmoid(logits)

    lane = lax.broadcasted_iota(jnp.int32, (tm, LANES), 1)
    lane_f = lane.astype(f32)
    neg = -jnp.inf
    sel = jnp.where(lane < N_EXPERTS, scores + rbias_ref[...], neg)
    grp = lane >> 2

    best = None
    for g in range(N_GROUPS):
        mg = jnp.where(grp == g, sel, neg)
        m1, i1 = _first_index_of_max(mg, lane_f)
        m2 = jnp.max(jnp.where(lane_f == i1, neg, mg), axis=-1, keepdims=True)
        gs = m1 + m2
        if best is None:
            best, gi = gs, jnp.zeros_like(gs)
        else:
            better = gs > best
            gi = jnp.where(better, float(g), gi)
            best = jnp.where(better, gs, best)

    mg = jnp.where(grp.astype(f32) == gi, sel, neg)
    _, e1 = _first_index_of_max(mg, lane_f)
    mg2 = jnp.where(lane_f == e1, neg, mg)
    _, e2 = _first_index_of_max(mg2, lane_f)
    w1 = jnp.sum(jnp.where(lane_f == e1, scores, 0.0), axis=-1, keepdims=True)
    w2 = jnp.sum(jnp.where(lane_f == e2, scores, 0.0), axis=-1, keepdims=True)
    den = w1 + w2
    w1, w2 = w1 / den, w2 / den

    first_lower = e1 < e2
    la = jnp.where(first_lower, e1, e2) - EXPERTS_PER_GROUP * gi
    lb = jnp.where(first_lower, e2, e1) - EXPERTS_PER_GROUP * gi
    w_a = jnp.where(first_lower, w1, w2)
    w_b = jnp.where(first_lower, w2, w1)
    cls = PAIRS_PER_GROUP * gi + la * (7.0 - la) * 0.5 + (lb - la - 1.0)

    row = (row0 + lax.broadcasted_iota(jnp.int32, (tm, 1), 0)).astype(f32)
    routed = (row - jnp.floor((row + 0.5) * (1.0 / tp)) * tp) < n_tok
    onehot = jnp.where((lane_f == cls) & routed, 1.0, 0.0)

    r_i = lax.broadcasted_iota(jnp.int32, (tm, tm), 0)
    c_i = lax.broadcasted_iota(jnp.int32, (tm, tm), 1)
    lower = jnp.where(c_i < r_i, 1.0, 0.0).astype(bf16)
    before = _dot(lower, onehot.astype(bf16)) + carry_ref[...]
    rank = jnp.sum(onehot * before, axis=-1, keepdims=True)
    carry_ref[...] += jnp.sum(onehot, axis=0, keepdims=True)

    cls_out = jnp.where(routed, cls, -1.0)
    ri_ref[rows, :] = jnp.where(lane == 0, cls_out, jnp.where(lane == 1, rank, 0.0)).astype(jnp.int32)

    h1p_ref[rows, :half] = _pack_bf16_pair(h1[:, :half], h1[:, half:])
    weights = jnp.where(lane == 0, w_a, jnp.where(lane == 1, w_b, 0.0))
    h1p_ref[rows, half:] = lax.bitcast_convert_type(weights, jnp.int32)


def _out_proj(o_a, o_b, h, w_out_b, g_a, g_b, ln_g, ln_b, wr_hl, rbias, layer, alpha, tp, n_tok):
    n, d = h.shape
    sub = _row_tile(n, 256)
    tm = 2 * sub if n % (2 * sub) == 0 else sub
    row_w = d // 2 + LANES
    wmap = lambda i: (layer, 0, 0)
    cmap = lambda i: (0, 0)
    rmap = lambda i: (i, 0)
    return pl.pallas_call(
        functools.partial(_out_proj_kernel, alpha=alpha, tp=tp, n_tok=n_tok, sub=sub),
        grid=(n // tm,),
        in_specs=[
            pl.BlockSpec((tm, MLA_WIDTH), rmap),
            pl.BlockSpec((tm, GQA_WIDTH), rmap),
            pl.BlockSpec((tm, d), rmap),
            _const_spec((1, MLA_WIDTH + GQA_WIDTH, d), wmap),
            _const_spec((1, 1, MLA_WIDTH), wmap),
            _const_spec((1, 1, GQA_WIDTH), wmap),
            _const_spec((1, 1, d), wmap),
            _const_spec((1, 1, d), wmap),
            _const_spec((d, 2 * LANES), cmap),
            _const_spec((1, LANES), cmap),
        ],
        out_specs=[
            pl.BlockSpec((tm, d), rmap),
            pl.BlockSpec((tm, row_w), rmap),
            pl.BlockSpec((tm, LANES), rmap),
            pl.BlockSpec((1, LANES), cmap),
        ],
        out_shape=[
            jax.ShapeDtypeStruct((n, d), jnp.float32),
            jax.ShapeDtypeStruct((n, row_w), jnp.int32),
            jax.ShapeDtypeStruct((n, LANES), jnp.int32),
            jax.ShapeDtypeStruct((1, LANES), jnp.float32),
        ],
        scratch_shapes=[pltpu.VMEM((1, LANES), jnp.float32)],
        compiler_params=_cparams(("arbitrary",)),
        name="out_proj",
    )(o_a, o_b, h, w_out_b, g_a, g_b, ln_g, ln_b, wr_hl, rbias)


SUBLANES = 8


def _start_row_copies(idx_ref, n_rows, make_copy):
    def issue(g, carry):
        for k in range(SUBLANES):
            idx = idx_ref[0, 0, g * SUBLANES + k]
            make_copy(g, k, lax.shift_right_logical(idx, 3), idx & (SUBLANES - 1)).start()
        return carry

    lax.fori_loop(0, n_rows // SUBLANES, issue, 0)


def _dispatch_kernel(pos_ref, x_ref, xs_init_hbm, xs_hbm, buf_ref, sem):
    del xs_init_hbm
    i = pl.program_id(0)
    last = pl.num_programs(0) - 1
    slot = i % 2
    tn = x_ref.shape[0]
    groups = tn // SUBLANES

    def wait_slot(s):
        pltpu.make_async_copy(buf_ref.at[s], xs_hbm.at[pl.ds(0, groups)], sem.at[s]).wait()

    @pl.when(i >= 2)
    def _():
        wait_slot(slot)

    buf_ref[slot] = x_ref[...].reshape(groups, SUBLANES, x_ref.shape[1])
    _start_row_copies(pos_ref, tn, lambda g, k, hi, lo: pltpu.make_async_copy(
        buf_ref.at[slot, g, pl.ds(k, 1), :], xs_hbm.at[hi, pl.ds(lo, 1), :], sem.at[slot]))

    @pl.when(i == last)
    def _():
        wait_slot(slot)

        @pl.when(last >= 1)
        def _():
            wait_slot(1 - slot)


def _dispatch(pos3, h1p, xs_init):
    n, row_w = h1p.shape
    tn = pos3.shape[2]
    return pl.pallas_call(
        _dispatch_kernel,
        grid=(n // tn,),
        in_specs=[
            pl.BlockSpec((1, 1, tn), lambda i: (i, 0, 0), memory_space=pltpu.SMEM),
            pl.BlockSpec((tn, row_w), lambda i: (i, 0)),
            pl.BlockSpec(memory_space=pl.ANY),
        ],
        out_specs=pl.BlockSpec(memory_space=pl.ANY),
        out_shape=jax.ShapeDtypeStruct(xs_init.shape, xs_init.dtype),
        scratch_shapes=[pltpu.VMEM((2, tn // SUBLANES, SUBLANES, row_w), h1p.dtype),
                        pltpu.SemaphoreType.DMA((2,))],
        input_output_aliases={2: 0},
        compiler_params=_cparams(("arbitrary",)),
        name="dispatch",
    )(pos3, h1p, xs_init)


def _experts_kernel(tile_e_ref, meta_ref, xs_ref, wg_ref, wu_ref, wd_ref, ys_ref, acc_ref):
    i = pl.program_id(0)
    k = pl.program_id(1)
    half = wg_ref.shape[1] // 2

    @pl.when(i < meta_ref[0])
    def _():
        lo, hi = _unpack_bf16_pair(xs_ref[:, :half])
        g = _dot(lo, wg_ref[0, :half, :]) + _dot(hi, wg_ref[0, half:, :])
        u = _dot(lo, wu_ref[0, :half, :]) + _dot(hi, wu_ref[0, half:, :])
        a = (g * jax.nn.sigmoid(g) * u).astype(jnp.bfloat16)
        y = _dot(a, wd_ref[0])
        second = (k + i) % 2 == 1
        wab = lax.bitcast_convert_type(xs_ref[:, half:], jnp.float32)
        y = y * jnp.where(second, wab[:, 1:2], wab[:, 0:1])

        @pl.when(k == 0)
        def _():
            acc_ref[...] = y

        @pl.when(k == 1)
        def _():
            ys_ref[...] = acc_ref[...] + y

    @pl.when((i >= meta_ref[0]) & (k == 1))
    def _():
        ys_ref[...] = jnp.zeros_like(ys_ref)


def _experts(tile_e, meta, xs, wg_b, wu_b, wd_b, max_tiles):
    tm = EXPERT_TILE
    row_w = xs.shape[1]
    d, ff = wg_b.shape[1], wg_b.shape[2]

    def xmap(i, k, te, mt):
        return (jnp.minimum(i, mt[0] - 1), 0)

    def wmap(i, k, te, mt):
        return (te[2 * i + k], 0, 0)

    grid_spec = pltpu.PrefetchScalarGridSpec(
        num_scalar_prefetch=2,
        grid=(max_tiles, 2),
        in_specs=[
            pl.BlockSpec((tm, row_w), xmap),
            pl.BlockSpec((1, d, ff), wmap),
            pl.BlockSpec((1, d, ff), wmap),
            pl.BlockSpec((1, ff, d), wmap),
        ],
        out_specs=pl.BlockSpec((tm, d), lambda i, k, te, mt: (i, 0)),
        scratch_shapes=[pltpu.VMEM((tm, d), jnp.float32)],
    )
    return pl.pallas_call(
        _experts_kernel,
        grid_spec=grid_spec,
        out_shape=jax.ShapeDtypeStruct((max_tiles * tm, d), jnp.float32),
        compiler_params=_cparams(("arbitrary", "arbitrary")),
        name="experts",
    )(tile_e, meta, xs, wg_b, wu_b, wd_b)


def _combine_kernel(pos_ref, pos_next_ref, ys_hbm, h1_ref, g_ref, b_ref, *rest, alpha, final):
    if final:
        out_ref, ybuf_ref, sem = rest
        i = pl.program_id(0) * pl.num_programs(1) + pl.program_id(1)
        last = pl.num_programs(0) * pl.num_programs(1) - 1
    else:
        h_ref, hb_ref, ybuf_ref, sem = rest
        i = pl.program_id(0)
        last = pl.num_programs(0) - 1
    slot = i % 2
    tn = h1_ref.shape[0]

    def start_gather(idx_ref, s):
        _start_row_copies(idx_ref, tn, lambda g, k, hi, lo: pltpu.make_async_copy(
            ys_hbm.at[hi, pl.ds(lo, 1), :], ybuf_ref.at[s, g, pl.ds(k, 1), :], sem.at[s]))

    @pl.when(i == 0)
    def _():
        start_gather(pos_ref, 0)

    pltpu.make_async_copy(ys_hbm.at[pl.ds(0, tn // SUBLANES)], ybuf_ref.at[slot], sem.at[slot]).wait()

    @pl.when(i < last)
    def _():
        start_gather(pos_next_ref, 1 - slot)

    gathered = ybuf_ref[slot].reshape(tn, h1_ref.shape[1])
    y = _layer_norm_rows(alpha * h1_ref[...] + gathered, g_ref[0], b_ref[0])
    if final:
        out_ref[0] = y
    else:
        h_ref[...] = y
        hb_ref[...] = y.astype(jnp.bfloat16)


def _combine_final(pos, ys, h1, ln_g, ln_b, layer, alpha, batch, seq, tp):
    n, d = h1.shape
    tn = tp - seq
    per_batch = tp // tn
    nx = seq // tn
    pos3 = pos.reshape(batch * per_batch, 1, tn)
    wmap = lambda b, j: (layer, 0, 0)
    cur = lambda b, j: b * per_batch + j

    def nxt(b, j):
        wrap = j + 1 >= nx
        return jnp.where(wrap, jnp.minimum(b + 1, batch - 1) * per_batch, b * per_batch + j + 1)

    return pl.pallas_call(
        functools.partial(_combine_kernel, alpha=alpha, final=True),
        grid=(batch, nx),
        in_specs=[
            pl.BlockSpec((1, 1, tn), lambda b, j: (cur(b, j), 0, 0), memory_space=pltpu.SMEM),
            pl.BlockSpec((1, 1, tn), lambda b, j: (nxt(b, j), 0, 0), memory_space=pltpu.SMEM),
            pl.BlockSpec(memory_space=pl.ANY),
            pl.BlockSpec((tn, d), lambda b, j: (cur(b, j), 0)),
            _const_spec((1, 1, d), wmap),
            _const_spec((1, 1, d), wmap),
        ],
        out_specs=pl.BlockSpec((1, tn, d), lambda b, j: (b, j, 0)),
        out_shape=jax.ShapeDtypeStruct((batch, seq, d), jnp.float32),
        scratch_shapes=[pltpu.VMEM((2, tn // SUBLANES, SUBLANES, d), jnp.float32),
                        pltpu.SemaphoreType.DMA((2,))],
        compiler_params=_cparams(("arbitrary", "arbitrary")),
        name="combine_final",
    )(pos3, pos3, ys, h1, ln_g, ln_b)


def _combine(pos, ys, h1, ln_g, ln_b, layer, alpha, tn):
    n, d = h1.shape
    steps = n // tn
    pos3 = pos.reshape(steps, 1, tn)
    wmap = lambda i: (layer, 0, 0)
    rmap = lambda i: (i, 0)
    return pl.pallas_call(
        functools.partial(_combine_kernel, alpha=alpha, final=False),
        grid=(steps,),
        in_specs=[
            pl.BlockSpec((1, 1, tn), lambda i: (i, 0, 0), memory_space=pltpu.SMEM),
            pl.BlockSpec((1, 1, tn), lambda i: (jnp.minimum(i + 1, steps - 1), 0, 0), memory_space=pltpu.SMEM),
            pl.BlockSpec(memory_space=pl.ANY),
            pl.BlockSpec((tn, d), rmap),
            _const_spec((1, 1, d), wmap),
            _const_spec((1, 1, d), wmap),
        ],
        out_specs=[pl.BlockSpec((tn, d), rmap), pl.BlockSpec((tn, d), rmap)],
        out_shape=[jax.ShapeDtypeStruct((n, d), jnp.float32), jax.ShapeDtypeStruct((n, d), jnp.bfloat16)],
        scratch_shapes=[pltpu.VMEM((2, tn // SUBLANES, SUBLANES, d), jnp.float32),
                        pltpu.SemaphoreType.DMA((2,))],
        compiler_params=_cparams(("arbitrary",)),
        name="combine",
    )(pos3, pos3, ys, h1, ln_g, ln_b)


def _rope_tables(seq, tp):
    rows = seq // GRID_W
    pad = tp - seq - N_META
    pos_row = jnp.concatenate([jnp.repeat(jnp.arange(rows, dtype=jnp.float32), GRID_W),
                               jnp.full((N_META,), -1.0, jnp.float32), jnp.zeros((pad,), jnp.float32)])
    pos_col = jnp.concatenate([jnp.tile(jnp.arange(GRID_W, dtype=jnp.float32), rows),
                               jnp.arange(N_META, dtype=jnp.float32), jnp.zeros((pad,), jnp.float32)])

    def tables(rot_dim):
        n = rot_dim // 4
        inv = ROPE_THETA ** (-jnp.arange(n, dtype=jnp.float32) / n)
        ang = jnp.concatenate([pos_row[:, None] * inv, pos_col[:, None] * inv], axis=-1)
        cos, sin = jnp.cos(ang), jnp.sin(ang)
        reps = LANES // rot_dim
        return (jnp.tile(jnp.concatenate([cos, cos], axis=-1), (1, reps)),
                jnp.tile(jnp.concatenate([-sin, sin], axis=-1), (1, reps)))

    cos_a, sin_a = tables(ROPE_DIM)
    cos_b, sin_b = tables(GQA_HEAD_DIM)
    return cos_a, sin_a, cos_b, sin_b


def _relayout_weights(w_in, w_q_b, w_kv_b):
    bf16 = jnp.bfloat16
    off_kr = Q_LORA + KV_LORA
    off_gq = off_kr + ROPE_DIM
    kr = w_in[:, :, off_kr:off_gq]
    w_in_p = jnp.concatenate([w_in[:, :, :off_kr], kr, kr, w_in[:, :, off_gq:]], axis=-1).astype(bf16)
    depth = w_in.shape[0]
    wq = w_q_b.reshape(depth, Q_LORA, MLA_HEADS, QK_HEAD)
    wq_p = jnp.concatenate([wq[..., :NOPE_DIM].reshape(depth, Q_LORA, -1),
                            wq[..., NOPE_DIM:].reshape(depth, Q_LORA, -1)], axis=-1).astype(bf16)
    wkv = w_kv_b.reshape(depth, KV_LORA, MLA_HEADS, NOPE_DIM + V_DIM)
    wkv_p = jnp.concatenate([wkv[..., :NOPE_DIM].reshape(depth, KV_LORA, -1),
                             wkv[..., NOPE_DIM:].reshape(depth, KV_LORA, -1)], axis=-1).astype(bf16)
    return w_in_p, wq_p, wkv_p


def _dispatch_plan(ri, counts, max_tiles, tp, n_tok):
    tm = EXPERT_TILE
    n = ri.shape[0]
    rows = max_tiles * tm
    cls, rank = ri[:, 0], ri[:, 1]
    cnt = counts[0, :N_CLASSES].astype(jnp.int32)
    tiles_c = (cnt + tm - 1) // tm
    tile_end = jnp.cumsum(tiles_c)
    tile_start = tile_end - tiles_c
    n_tiles = tile_end[-1]
    routed = cls >= 0
    tok = jnp.arange(n, dtype=jnp.int32)
    spare = rows + (tok // tp) * (tp - n_tok) + (tok % tp - n_tok)
    sorted_pos = tile_start[jnp.maximum(cls, 0)] * tm + rank
    pos_scatter = jnp.where(routed, sorted_pos, spare).astype(jnp.int32)
    pos_gather = jnp.where(routed, sorted_pos, 0).astype(jnp.int32)

    t = jnp.minimum(jnp.arange(max_tiles, dtype=jnp.int32), n_tiles - 1)
    tile_cls = jnp.sum((t[:, None] >= tile_end[None, :]).astype(jnp.int32), axis=1)
    tile_cls = jnp.minimum(tile_cls, N_CLASSES - 1)
    pair_lo = jnp.array([0, 0, 0, 1, 1, 2], jnp.int32)
    pair_hi = jnp.array([1, 2, 3, 2, 3, 3], jnp.int32)
    e_a = EXPERTS_PER_GROUP * (tile_cls // PAIRS_PER_GROUP) + pair_lo[tile_cls % PAIRS_PER_GROUP]
    e_b = EXPERTS_PER_GROUP * (tile_cls // PAIRS_PER_GROUP) + pair_hi[tile_cls % PAIRS_PER_GROUP]
    odd = (jnp.arange(max_tiles) % 2) == 1
    tile_e = jnp.stack([jnp.where(odd, e_b, e_a), jnp.where(odd, e_a, e_b)], axis=-1).reshape(-1)
    step_tile = jnp.arange(2 * max_tiles) // 2
    tile_e = jnp.where(step_tile >= n_tiles, tile_e[2 * n_tiles - 1], tile_e).astype(jnp.int32)
    meta = jnp.stack([n_tiles, n_tiles]).astype(jnp.int32)
    return tile_e, meta, pos_scatter, pos_gather


def kernel(x, meta_tokens, ln_in_g, ln_in_b, w_in, g_q_lora, w_q_b, g_kv_lora, w_kv_b, g_qk_q, g_qk_k,
           g_out_mla, g_out_gqa, w_out, ln1_g, ln1_b, w_router, router_bias, w_gate, w_up, w_down,
           ln2_g, ln2_b):
    batch, seq, d = x.shape
    depth = w_in.shape[0]
    n_tok = seq + N_META
    tp = -(-n_tok // LANES) * LANES
    n = batch * tp
    alpha = (2.0 * depth) ** 0.25
    bf16 = jnp.bfloat16
    f32 = jnp.float32

    tabs = _rope_tables(seq, tp)
    w_in_p, wq_p, wkv_p = _relayout_weights(w_in, w_q_b, w_kv_b)
    w_out_b = w_out.astype(bf16)
    n_exp, ff = w_gate.shape[1], w_gate.shape[3]
    wg_rows = w_gate.reshape(depth * n_exp * d, ff)
    wu_rows = w_up.reshape(depth * n_exp * d, ff)
    wd_rows = w_down.reshape(depth * n_exp * ff, d)
    wr = jnp.pad(w_router.astype(f32), ((0, 0), (0, LANES - N_EXPERTS)))
    wr_hi = wr.astype(bf16)
    wr_lo = (wr - wr_hi.astype(f32)).astype(bf16)
    wr_hl = jnp.concatenate([wr_hi, wr_lo], axis=1)
    rbias = jnp.pad(router_bias.astype(f32), (0, LANES - N_EXPERTS)).reshape(1, LANES)
    row3 = lambda a: a.reshape(depth, 1, a.shape[-1])

    h, hb = _ln_in(x, meta_tokens.astype(x.dtype), ln_in_g.reshape(1, d), ln_in_b.reshape(1, d), tp)

    tm_e = EXPERT_TILE
    max_tiles = -(-(batch * n_tok) // tm_e) + N_CLASSES
    spare_rows = -(-(batch * (tp - n_tok)) // tm_e) * tm_e
    tn = _row_tile(n, 256)
    xs_rows, row_w = max_tiles * tm_e + spare_rows, d // 2 + LANES
    xs = jnp.zeros((xs_rows // SUBLANES, SUBLANES, row_w), jnp.int32)
    for l in range(depth):
        q_a, k_a, v_a, q_b, k_b, v_b = _in_proj(
            hb, w_in_p, wq_p, wkv_p, row3(g_q_lora), row3(g_kv_lora), row3(g_qk_q), row3(g_qk_k),
            tabs, l, batch, tp)
        o_a, (wg_b, wu_b) = _attention(q_a, k_a, v_a, n_tok, [(wg_rows, (l, depth)), (wu_rows, (l, depth))])
        o_b, (wd_b,) = _attention(q_b, k_b, v_b, n_tok, [(wd_rows, (l, depth))])
        wg_b, wu_b = wg_b.reshape(n_exp, d, ff), wu_b.reshape(n_exp, d, ff)
        wd_b = wd_b.reshape(n_exp, ff, d)
        h1, h1p, ri, counts = _out_proj(
            o_a, o_b, h, w_out_b, row3(g_out_mla), row3(g_out_gqa), row3(ln1_g), row3(ln1_b),
            wr_hl, rbias, l, alpha, tp, n_tok)
        tile_e, tmeta, pos_scatter, pos_gather = _dispatch_plan(ri, counts, max_tiles, tp, n_tok)
        xs = _dispatch(pos_scatter.reshape(n // tn, 1, tn), h1p, xs)
        ys = _experts(tile_e, tmeta, xs.reshape(xs_rows, row_w), wg_b, wu_b, wd_b, max_tiles)
        ys = ys.reshape(ys.shape[0] // SUBLANES, SUBLANES, d)
        if l + 1 < depth:
            h, hb = _combine(pos_gather, ys, h1, row3(ln2_g), row3(ln2_b), l, alpha, tn)
        else:
            out = _combine_final(pos_gather, ys, h1, row3(ln2_g), row3(ln2_b), l, alpha, batch, seq, tp)
    return out
```

```python
import functools
import math

import jax
import jax.numpy as jnp
from jax import lax
from jax.experimental import pallas as pl
from jax.experimental.pallas import tpu as pltpu

N_META = 16
GRID_W = 64
ROPE_THETA = 10000.0
EPS = 1e-6

MLA_HEADS = 8
Q_LORA = 512
KV_LORA = 256
NOPE_DIM = 128
ROPE_DIM = 64
V_DIM = 128
QK_HEAD = NOPE_DIM + ROPE_DIM
MLA_WIDTH = MLA_HEADS * V_DIM
LOG2_E = math.log2(math.e)
MLA_SCALE = LOG2_E / math.sqrt(QK_HEAD)
MLA_QK_PAD = 256

GQA_HEADS = 8
GQA_KV_HEADS = 2
GQA_HEAD_DIM = 128
GQA_WIDTH = GQA_HEADS * GQA_HEAD_DIM
GQA_SCALE = LOG2_E / math.sqrt(GQA_HEAD_DIM)

N_EXPERTS = 16
N_GROUPS = 4
EXPERTS_PER_GROUP = 4
PAIRS_PER_GROUP = 6
N_CLASSES = N_GROUPS * PAIRS_PER_GROUP
EXPERT_FF = 1024

LANES = 128
VMEM_LIMIT_BYTES = 58 * 1024 * 1024
EXPERT_TILE = 256
NEG_BIG = -1e30
HI16 = -65536

C_CQ = 0
C_CKV = C_CQ + Q_LORA
C_KR = C_CKV + KV_LORA
C_GQ = C_KR + LANES
C_GK = C_GQ + GQA_WIDTH
C_GV = C_GK + GQA_KV_HEADS * GQA_HEAD_DIM
IN_COLS_PAD = C_GV + GQA_KV_HEADS * GQA_HEAD_DIM


def _cparams(semantics, flags=None):
    return pltpu.CompilerParams(dimension_semantics=semantics, vmem_limit_bytes=VMEM_LIMIT_BYTES, flags=flags)


def _const_spec(block_shape, index_map):
    return pl.BlockSpec(block_shape, index_map, pipeline_mode=pl.Buffered(1))


def _row_tile(n, cap, mult=16):
    best = mult
    for t in range(mult, cap + 1, mult):
        if n % t == 0:
            best = t
    assert n % best == 0
    return best


def _layer_norm_rows(z, g, b):
    mu = jnp.mean(z, axis=-1, keepdims=True)
    zc = z - mu
    var = jnp.mean(zc * zc, axis=-1, keepdims=True)
    return zc * lax.rsqrt(var + EPS) * g + b


def _rms_rows(z, g):
    return z * lax.rsqrt(jnp.mean(z * z, axis=-1, keepdims=True) + EPS) * g


def _dot(a, b):
    return jnp.dot(a, b, preferred_element_type=jnp.float32)


def _pack_bf16_pair(lo_f32, hi_f32):
    lo_bits = lax.bitcast_convert_type(lo_f32.astype(jnp.bfloat16).astype(jnp.float32), jnp.int32)
    hi_bits = lax.bitcast_convert_type(hi_f32.astype(jnp.bfloat16).astype(jnp.float32), jnp.int32)
    return lax.shift_right_logical(lo_bits, 16) | (hi_bits & HI16)


def _unpack_bf16_pair(words):
    lo = lax.bitcast_convert_type(words << 16, jnp.float32).astype(jnp.bfloat16)
    hi = lax.bitcast_convert_type(words & HI16, jnp.float32).astype(jnp.bfloat16)
    return lo, hi


def _ln_in_kernel(x_ref, meta_ref, g_ref, b_ref, h_ref, hb_ref):
    j = pl.program_id(1)
    last = pl.num_programs(1) - 1

    def emit(rows):
        y = _layer_norm_rows(rows, g_ref[...], b_ref[...])
        h_ref[...] = y
        hb_ref[...] = y.astype(jnp.bfloat16)

    @pl.when(j < last)
    def _():
        emit(x_ref[0])

    @pl.when(j == last)
    def _():
        meta = meta_ref[...]
        pad = jnp.zeros((h_ref.shape[0] - meta.shape[0], meta.shape[1]), meta.dtype)
        emit(jnp.concatenate([meta, pad], axis=0))


def _ln_in(x, meta_tokens, g, b, tp):
    batch, seq, d = x.shape
    tm = tp - seq
    assert seq % tm == 0 and meta_tokens.shape[0] <= tm
    nt = tp // tm
    n = batch * tp
    omap = lambda bi, j: (bi * nt + j, 0)
    cmap = lambda bi, j: (0, 0)
    return pl.pallas_call(
        _ln_in_kernel,
        grid=(batch, nt),
        in_specs=[
            pl.BlockSpec((1, tm, d), lambda bi, j: (bi, jnp.minimum(j, nt - 2), 0)),
            pl.BlockSpec(meta_tokens.shape, cmap),
            pl.BlockSpec((1, d), cmap),
            pl.BlockSpec((1, d), cmap),
        ],
        out_specs=[pl.BlockSpec((tm, d), omap), pl.BlockSpec((tm, d), omap)],
        out_shape=[jax.ShapeDtypeStruct((n, d), jnp.float32), jax.ShapeDtypeStruct((n, d), jnp.bfloat16)],
        compiler_params=_cparams(("parallel", "parallel")),
        name="ln_in",
    )(x, meta_tokens, g, b)


def _swap_halves_64(x):
    lane = lax.broadcasted_iota(jnp.int32, x.shape, 1)
    fwd = pltpu.roll(x, 32, 1)
    bwd = pltpu.roll(x, 96, 1)
    return jnp.where((lane & 63) < 32, bwd, fwd)


def _in_proj_kernel(x_ref, w_in_ref, wq_ref, wkv_ref, gq_ref, gkv_ref, gqq_ref, gqk_ref,
                    cos_a_ref, sin_a_ref, cos_b_ref, sin_b_ref,
                    q_mla_ref, k_mla_ref, v_mla_ref, q_gqa_ref, k_gqa_ref, v_gqa_ref):
    bf16 = jnp.bfloat16
    u = _dot(x_ref[...], w_in_ref[0])
    cos_a, sin_a = cos_a_ref[...], sin_a_ref[...]
    cos_b, sin_b = cos_b_ref[...], sin_b_ref[...]
    lane = lax.broadcasted_iota(jnp.int32, cos_a.shape, 1)

    c_q = _rms_rows(u[:, C_CQ:C_CKV], gq_ref[0]).astype(bf16)
    qa = _dot(c_q, wq_ref[0])
    c_kv = _rms_rows(u[:, C_CKV:C_KR], gkv_ref[0]).astype(bf16)
    kv = _dot(c_kv, wkv_ref[0])
    kr = u[:, C_KR:C_GQ]
    kr = kr * cos_a + _swap_halves_64(kr) * sin_a
    rope_base = MLA_HEADS * NOPE_DIM
    for pair in range(MLA_HEADS // 2):
        blk = qa[:, rope_base + LANES * pair:rope_base + LANES * (pair + 1)]
        rot = blk * cos_a + _swap_halves_64(blk) * sin_a
        for half in range(2):
            h = 2 * pair + half
            own = jnp.where((lane >> 6) == half, rot, 0.0)
            qh = jnp.concatenate([qa[:, NOPE_DIM * h:NOPE_DIM * (h + 1)], own], axis=1) * MLA_SCALE
            q_mla_ref[0, h] = qh.astype(bf16)
    for h in range(MLA_HEADS):
        kh = jnp.concatenate([kv[:, NOPE_DIM * h:NOPE_DIM * (h + 1)], kr], axis=1)
        k_mla_ref[0, h] = kh.astype(bf16)
        v0 = MLA_HEADS * NOPE_DIM + V_DIM * h
        v_mla_ref[0, h] = kv[:, v0:v0 + V_DIM].astype(bf16)

    for h in range(GQA_HEADS):
        qh = _rms_rows(u[:, C_GQ + GQA_HEAD_DIM * h:C_GQ + GQA_HEAD_DIM * (h + 1)], gqq_ref[0])
        qh = qh * cos_b + pltpu.roll(qh, 64, 1) * sin_b
        q_gqa_ref[0, h] = (qh * GQA_SCALE).astype(bf16)
    for h in range(GQA_KV_HEADS):
        kh = _rms_rows(u[:, C_GK + GQA_HEAD_DIM * h:C_GK + GQA_HEAD_DIM * (h + 1)], gqk_ref[0])
        kh = kh * cos_b + pltpu.roll(kh, 64, 1) * sin_b
        k_gqa_ref[0, h] = kh.astype(bf16)
        v_gqa_ref[0, h] = u[:, C_GV + GQA_HEAD_DIM * h:C_GV + GQA_HEAD_DIM * (h + 1)].astype(bf16)


def _in_proj(hb, w_in_p, wq_p, wkv_p, g_q, g_kv, g_qq, g_qk, tabs, layer, batch, tp):
    n, d = hb.shape
    tm = _row_tile(tp, 320)
    nt = tp // tm
    cos_a, sin_a, cos_b, sin_b = tabs
    wmap = lambda b, i: (layer, 0, 0)
    tmap = lambda b, i: (i, 0)
    omap = lambda b, i: (b, 0, i, 0)
    bf16 = jnp.bfloat16
    return pl.pallas_call(
        _in_proj_kernel,
        grid=(batch, nt),
        in_specs=[
            pl.BlockSpec((tm, d), lambda b, i: (b * nt + i, 0)),
            _const_spec((1, d, IN_COLS_PAD), wmap),
            _const_spec((1, Q_LORA, wq_p.shape[2]), wmap),
            _const_spec((1, KV_LORA, wkv_p.shape[2]), wmap),
            _const_spec((1, 1, Q_LORA), wmap),
            _const_spec((1, 1, KV_LORA), wmap),
            _const_spec((1, 1, GQA_HEAD_DIM), wmap),
            _const_spec((1, 1, GQA_HEAD_DIM), wmap),
            pl.BlockSpec((tm, LANES), tmap),
            pl.BlockSpec((tm, LANES), tmap),
            pl.BlockSpec((tm, LANES), tmap),
            pl.BlockSpec((tm, LANES), tmap),
        ],
        out_specs=[
            pl.BlockSpec((1, MLA_HEADS, tm, MLA_QK_PAD), omap),
            pl.BlockSpec((1, MLA_HEADS, tm, MLA_QK_PAD), omap),
            pl.BlockSpec((1, MLA_HEADS, tm, V_DIM), omap),
            pl.BlockSpec((1, GQA_HEADS, tm, GQA_HEAD_DIM), omap),
            pl.BlockSpec((1, GQA_KV_HEADS, tm, GQA_HEAD_DIM), omap),
            pl.BlockSpec((1, GQA_KV_HEADS, tm, GQA_HEAD_DIM), omap),
        ],
        out_shape=[
            jax.ShapeDtypeStruct((batch, MLA_HEADS, tp, MLA_QK_PAD), bf16),
            jax.ShapeDtypeStruct((batch, MLA_HEADS, tp, MLA_QK_PAD), bf16),
            jax.ShapeDtypeStruct((batch, MLA_HEADS, tp, V_DIM), bf16),
            jax.ShapeDtypeStruct((batch, GQA_HEADS, tp, GQA_HEAD_DIM), bf16),
            jax.ShapeDtypeStruct((batch, GQA_KV_HEADS, tp, GQA_HEAD_DIM), bf16),
            jax.ShapeDtypeStruct((batch, GQA_KV_HEADS, tp, GQA_HEAD_DIM), bf16),
        ],
        compiler_params=_cparams(("parallel", "parallel")),
        name="in_proj",
    )(hb, w_in_p, wq_p, wkv_p, g_q, g_kv, g_qq, g_qk, cos_a, sin_a, cos_b, sin_b)


def _attn_kernel(q_ref, k_ref, v_ref, *rest, tq, n_keys, n_cast):
    o_ref = rest[n_cast]
    for src_ref, dst_ref in zip(rest[:n_cast], rest[n_cast + 1:]):
        dst_ref[...] = src_ref[...].astype(dst_ref.dtype)
    tp = k_ref.shape[2]
    body = tp - LANES
    nt = (((1,), (1,)), ((), ()))
    k_body, k_tail = k_ref[0, 0, :body, :], k_ref[0, 0, body:, :]
    v_body, v_tail = v_ref[0, 0, :body, :], v_ref[0, 0, body:, :]
    tail_ok = lax.broadcasted_iota(jnp.int32, (1, LANES), 1) < (n_keys - body)
    n_q = -(-n_keys // 16) * 16
    if n_q < tp:
        o_ref[n_q:, :] = jnp.zeros((tp - n_q, o_ref.shape[1]), o_ref.dtype)
    for r0 in range(0, n_q, tq):
        rows = slice(r0, min(r0 + tq, n_q))
        q = q_ref[0, 0, rows, :]
        s_body = lax.dot_general(q, k_body, nt, preferred_element_type=jnp.float32)
        s_tail = lax.dot_general(q, k_tail, nt, preferred_element_type=jnp.float32)
        s_tail = jnp.where(tail_ok, s_tail, NEG_BIG)
        m = jnp.maximum(jnp.max(s_body, axis=-1, keepdims=True), jnp.max(s_tail, axis=-1, keepdims=True))
        p_body = jnp.exp2(s_body - m)
        p_tail = jnp.exp2(s_tail - m)
        l = jnp.sum(p_body, axis=-1, keepdims=True) + jnp.sum(p_tail, axis=-1, keepdims=True)
        o = _dot(p_body.astype(jnp.bfloat16), v_body) + _dot(p_tail.astype(jnp.bfloat16), v_tail)
        o_ref[rows, :] = (o / l).astype(o_ref.dtype)


def _attention(q, k, v, n_keys, casts):
    batch, hq, tp, dk = q.shape
    hk, dv = k.shape[1], v.shape[3]
    rep = hq // hk
    tq = _row_tile(tp, 320)
    steps = batch * hq
    cast_in, cast_out, cast_shapes = [], [], []
    for w, layer in casts:
        rows = w.shape[0] // (layer[1])
        assert rows % steps == 0
        blk = rows // steps
        off = layer[0] * steps
        cast_in.append(pl.BlockSpec((blk, w.shape[1]), lambda b, h, off=off: (off + b * hq + h, 0)))
        cast_out.append(pl.BlockSpec((blk, w.shape[1]), lambda b, h: (b * hq + h, 0)))
        cast_shapes.append(jax.ShapeDtypeStruct((rows, w.shape[1]), jnp.bfloat16))
    outs = pl.pallas_call(
        functools.partial(_attn_kernel, tq=tq, n_keys=n_keys, n_cast=len(casts)),
        grid=(batch, hq),
        in_specs=[
            pl.BlockSpec((1, 1, tp, dk), lambda b, h: (b, h, 0, 0)),
            pl.BlockSpec((1, 1, tp, dk), lambda b, h: (b, h // rep, 0, 0)),
            pl.BlockSpec((1, 1, tp, dv), lambda b, h: (b, h // rep, 0, 0)),
        ] + cast_in,
        out_specs=[pl.BlockSpec((tp, dv), lambda b, h: (b, h))] + cast_out,
        out_shape=[jax.ShapeDtypeStruct((batch * tp, hq * dv), jnp.bfloat16)] + cast_shapes,
        compiler_params=_cparams(("parallel", "parallel")),
        name="attention",
    )(q, k, v, *[w for w, _ in casts])
    return outs[0], outs[1:]


def _first_index_of_max(vals, lane_f):
    m = jnp.max(vals, axis=-1, keepdims=True)
    idx = jnp.min(jnp.where(vals == m, lane_f, float(LANES)), axis=-1, keepdims=True)
    return m, idx


def _out_proj_kernel(oa_ref, ob_ref, h_ref, w_out_ref, ga_ref, gb_ref, lg_ref, lb_ref,
                     wr_hl_ref, rbias_ref,
                     h1_ref, h1p_ref, ri_ref, cnt_ref, carry_ref, *, alpha, tp, n_tok, sub):
    i = pl.program_id(0)
    tm = h_ref.shape[0]

    @pl.when(i == 0)
    def _():
        carry_ref[...] = jnp.zeros_like(carry_ref)

    blocks = [slice(r0, r0 + sub) for r0 in range(0, tm, sub)]
    mixed = [_out_proj_matmul(rows, oa_ref, ob_ref, w_out_ref, ga_ref, gb_ref) for rows in blocks]
    routed_in = [_out_proj_norm(rows, m, h_ref, lg_ref, lb_ref, wr_hl_ref, h1_ref, alpha=alpha)
                 for rows, m in zip(blocks, mixed)]
    for rows, (h1, logits) in zip(blocks, routed_in):
        _out_proj_route(rows, i * tm + rows.start, h1, logits, rbias_ref, h1p_ref, ri_ref, carry_ref,
                        tp=tp, n_tok=n_tok)
    cnt_ref[...] = carry_ref[...]


def _out_proj_matmul(rows, oa_ref, ob_ref, w_out_ref, ga_ref, gb_ref):
    bf16 = jnp.bfloat16
    na = _rms_rows(oa_ref[rows, :].astype(jnp.float32), ga_ref[0]).astype(bf16)
    nb = _rms_rows(ob_ref[rows, :].astype(jnp.float32), gb_ref[0]).astype(bf16)
    return _dot(na, w_out_ref[0, :MLA_WIDTH, :]) + _dot(nb, w_out_ref[0, MLA_WIDTH:, :])


def _out_proj_norm(rows, mixed, h_ref, lg_ref, lb_ref, wr_hl_ref, h1_ref, *, alpha):
    bf16 = jnp.bfloat16
    h1 = _layer_norm_rows(alpha * h_ref[rows, :] + mixed, lg_ref[0], lb_ref[0])
    h1_ref[rows, :] = h1
    hi = h1.astype(bf16)
    lo = (h1 - hi.astype(jnp.float32)).astype(bf16)
    hi_prod = _dot(hi, wr_hl_ref[...])
    logits = hi_prod[:, :LANES] + hi_prod[:, LANES:] + _dot(lo, wr_hl_ref[:, :LANES])
    return h1, logits


def _out_proj_route(rows, row0, h1, logits, rbias_ref, h1p_ref, ri_ref, carry_ref, *, tp, n_tok):
    bf16 = jnp.bfloat16
    f32 = jnp.float32
    tm = rows.stop - rows.start
    half = h1.shape[1] // 2
    scores = jax.nn.sigmoid(logits)

    lane = lax.broadcasted_iota(jnp.int32, (tm, LANES), 1)
    lane_f = lane.astype(f32)
    neg = -jnp.inf
    sel = jnp.where(lane < N_EXPERTS, scores + rbias_ref[...], neg)
    grp = lane >> 2

    best = None
    for g in range(N_GROUPS):
        mg = jnp.where(grp == g, sel, neg)
        m1, i1 = _first_index_of_max(mg, lane_f)
        m2 = jnp.max(jnp.where(lane_f == i1, neg, mg), axis=-1, keepdims=True)
        gs = m1 + m2
        if best is None:
            best, gi = gs, jnp.zeros_like(gs)
        else:
            better = gs > best
            gi = jnp.where(better, float(g), gi)
            best = jnp.where(better, gs, best)

    mg = jnp.where(grp.astype(f32) == gi, sel, neg)
    _, e1 = _first_index_of_max(mg, lane_f)
    mg2 = jnp.where(lane_f == e1, neg, mg)
    _, e2 = _first_index_of_max(mg2, lane_f)
    w1 = jnp.sum(jnp.where(lane_f == e1, scores, 0.0), axis=-1, keepdims=True)
    w2 = jnp.sum(jnp.where(lane_f == e2, scores, 0.0), axis=-1, keepdims=True)
    den = w1 + w2
    w1, w2 = w1 / den, w2 / den

    first_lower = e1 < e2
    la = jnp.where(first_lower, e1, e2) - EXPERTS_PER_GROUP * gi
    lb = jnp.where(first_lower, e2, e1) - EXPERTS_PER_GROUP * gi
    w_a = jnp.where(first_lower, w1, w2)
    w_b = jnp.where(first_lower, w2, w1)
    cls = PAIRS_PER_GROUP * gi + la * (7.0 - la) * 0.5 + (lb - la - 1.0)

    row = (row0 + lax.broadcasted_iota(jnp.int32, (tm, 1), 0)).astype(f32)
    routed = (row - jnp.floor((row + 0.5) * (1.0 / tp)) * tp) < n_tok
    onehot = jnp.where((lane_f == cls) & routed, 1.0, 0.0)

    r_i = lax.broadcasted_iota(jnp.int32, (tm, tm), 0)
    c_i = lax.broadcasted_iota(jnp.int32, (tm, tm), 1)
    lower = jnp.where(c_i < r_i, 1.0, 0.0).astype(bf16)
    before = _dot(lower, onehot.astype(bf16)) + carry_ref[...]
    rank = jnp.sum(onehot * before, axis=-1, keepdims=True)
    carry_ref[...] += jnp.sum(onehot, axis=0, keepdims=True)

    cls_out = jnp.where(routed, cls, -1.0)
    ri_ref[rows, :] = jnp.where(lane == 0, cls_out, jnp.where(lane == 1, rank, 0.0)).astype(jnp.int32)

    h1p_ref[rows, :half] = _pack_bf16_pair(h1[:, :half], h1[:, half:])
    weights = jnp.where(lane == 0, w_a, jnp.where(lane == 1, w_b, 0.0))
    h1p_ref[rows, half:] = lax.bitcast_convert_type(weights, jnp.int32)


def _out_proj(o_a, o_b, h, w_out_b, g_a, g_b, ln_g, ln_b, wr_hl, rbias, layer, alpha, tp, n_tok):
    n, d = h.shape
    sub = _row_tile(n, 256)
    tm = 2 * sub if n % (2 * sub) == 0 else sub
    row_w = d // 2 + LANES
    wmap = lambda i: (layer, 0, 0)
    cmap = lambda i: (0, 0)
    rmap = lambda i: (i, 0)
    return pl.pallas_call(
        functools.partial(_out_proj_kernel, alpha=alpha, tp=tp, n_tok=n_tok, sub=sub),
        grid=(n // tm,),
        in_specs=[
            pl.BlockSpec((tm, MLA_WIDTH), rmap),
            pl.BlockSpec((tm, GQA_WIDTH), rmap),
            pl.BlockSpec((tm, d), rmap),
            _const_spec((1, MLA_WIDTH + GQA_WIDTH, d), wmap),
            _const_spec((1, 1, MLA_WIDTH), wmap),
            _const_spec((1, 1, GQA_WIDTH), wmap),
            _const_spec((1, 1, d), wmap),
            _const_spec((1, 1, d), wmap),
            _const_spec((d, 2 * LANES), cmap),
            _const_spec((1, LANES), cmap),
        ],
        out_specs=[
            pl.BlockSpec((tm, d), rmap),
            pl.BlockSpec((tm, row_w), rmap),
            pl.BlockSpec((tm, LANES), rmap),
            pl.BlockSpec((1, LANES), cmap),
        ],
        out_shape=[
            jax.ShapeDtypeStruct((n, d), jnp.float32),
            jax.ShapeDtypeStruct((n, row_w), jnp.int32),
            jax.ShapeDtypeStruct((n, LANES), jnp.int32),
            jax.ShapeDtypeStruct((1, LANES), jnp.float32),
        ],
        scratch_shapes=[pltpu.VMEM((1, LANES), jnp.float32)],
        compiler_params=_cparams(("arbitrary",)),
        name="out_proj",
    )(o_a, o_b, h, w_out_b, g_a, g_b, ln_g, ln_b, wr_hl, rbias)


SUBLANES = 8


def _start_row_copies(idx_ref, n_rows, make_copy):
    def issue(g, carry):
        for k in range(SUBLANES):
            idx = idx_ref[0, 0, g * SUBLANES + k]
            make_copy(g, k, lax.shift_right_logical(idx, 3), idx & (SUBLANES - 1)).start()
        return carry

    lax.fori_loop(0, n_rows // SUBLANES, issue, 0)


def _dispatch_kernel(pos_ref, x_ref, xs_init_hbm, xs_hbm, buf_ref, sem):
    del xs_init_hbm
    i = pl.program_id(0)
    last = pl.num_programs(0) - 1
    slot = i % 2
    tn = x_ref.shape[0]
    groups = tn // SUBLANES

    def wait_slot(s):
        pltpu.make_async_copy(buf_ref.at[s], xs_hbm.at[pl.ds(0, groups)], sem.at[s]).wait()

    @pl.when(i >= 2)
    def _():
        wait_slot(slot)

    buf_ref[slot] = x_ref[...].reshape(groups, SUBLANES, x_ref.shape[1])
    _start_row_copies(pos_ref, tn, lambda g, k, hi, lo: pltpu.make_async_copy(
        buf_ref.at[slot, g, pl.ds(k, 1), :], xs_hbm.at[hi, pl.ds(lo, 1), :], sem.at[slot]))

    @pl.when(i == last)
    def _():
        wait_slot(slot)

        @pl.when(last >= 1)
        def _():
            wait_slot(1 - slot)


def _dispatch(pos3, h1p, xs_init):
    n, row_w = h1p.shape
    tn = pos3.shape[2]
    return pl.pallas_call(
        _dispatch_kernel,
        grid=(n // tn,),
        in_specs=[
            pl.BlockSpec((1, 1, tn), lambda i: (i, 0, 0), memory_space=pltpu.SMEM),
            pl.BlockSpec((tn, row_w), lambda i: (i, 0)),
            pl.BlockSpec(memory_space=pl.ANY),
        ],
        out_specs=pl.BlockSpec(memory_space=pl.ANY),
        out_shape=jax.ShapeDtypeStruct(xs_init.shape, xs_init.dtype),
        scratch_shapes=[pltpu.VMEM((2, tn // SUBLANES, SUBLANES, row_w), h1p.dtype),
                        pltpu.SemaphoreType.DMA((2,))],
        input_output_aliases={2: 0},
        compiler_params=_cparams(("arbitrary",)),
        name="dispatch",
    )(pos3, h1p, xs_init)


def _experts_kernel(plan_ref, meta_ref, xs_ref, wg_hbm, wu_hbm, wd_hbm, ys_ref, acc_ref, wg_buf, wu_buf, wd_buf,
                    sem):
    i = pl.program_id(0)
    k = pl.program_id(1)
    steps = 2 * pl.num_programs(0)
    t = 2 * i + k
    half = wg_buf.shape[1] // 2

    def weight_copies(e, s):
        return (pltpu.make_async_copy(wg_hbm.at[e], wg_buf.at[s], sem.at[s]),
                pltpu.make_async_copy(wu_hbm.at[e], wu_buf.at[s], sem.at[s]),
                pltpu.make_async_copy(wd_hbm.at[e], wd_buf.at[s], sem.at[s]))

    @pl.when(i < meta_ref[0])
    def _():
        slot = plan_ref[2 * steps + t]

        @pl.when(plan_ref[steps + t] == 1)
        def _():
            @pl.when(t == 0)
            def _():
                for c in weight_copies(plan_ref[0], slot):
                    c.start()

            for c in weight_copies(plan_ref[t], slot):
                c.wait()
            nxt = plan_ref[3 * steps + t]

            @pl.when(nxt >= 0)
            def _():
                for c in weight_copies(nxt, 1 - slot):
                    c.start()

        lo, hi = _unpack_bf16_pair(xs_ref[:, :half])
        g = _dot(lo, wg_buf[slot, :half, :]) + _dot(hi, wg_buf[slot, half:, :])
        u = _dot(lo, wu_buf[slot, :half, :]) + _dot(hi, wu_buf[slot, half:, :])
        a = (g * jax.nn.sigmoid(g) * u).astype(jnp.bfloat16)
        y = _dot(a, wd_buf[slot])
        second = (k + i) % 2 == 1
        wab = lax.bitcast_convert_type(xs_ref[:, half:], jnp.float32)
        y = y * jnp.where(second, wab[:, 1:2], wab[:, 0:1])

        @pl.when(k == 0)
        def _():
            acc_ref[...] = y

        @pl.when(k == 1)
        def _():
            ys_ref[...] = acc_ref[...] + y

    @pl.when((i >= meta_ref[0]) & (k == 1))
    def _():
        ys_ref[...] = jnp.zeros_like(ys_ref)


def _experts(plan, meta, xs, wg_b, wu_b, wd_b, max_tiles):
    tm = EXPERT_TILE
    row_w = xs.shape[1]
    d, ff = wg_b.shape[1], wg_b.shape[2]

    def xmap(i, k, pr, mt):
        return (jnp.minimum(i, mt[0] - 1), 0)

    grid_spec = pltpu.PrefetchScalarGridSpec(
        num_scalar_prefetch=2,
        grid=(max_tiles, 2),
        in_specs=[
            pl.BlockSpec((tm, row_w), xmap),
            pl.BlockSpec(memory_space=pl.ANY),
            pl.BlockSpec(memory_space=pl.ANY),
            pl.BlockSpec(memory_space=pl.ANY),
        ],
        out_specs=pl.BlockSpec((tm, d), lambda i, k, pr, mt: (i, 0)),
        scratch_shapes=[
            pltpu.VMEM((tm, d), jnp.float32),
            pltpu.VMEM((2, d, ff), wg_b.dtype),
            pltpu.VMEM((2, d, ff), wu_b.dtype),
            pltpu.VMEM((2, ff, d), wd_b.dtype),
            pltpu.SemaphoreType.DMA((2,)),
        ],
    )
    return pl.pallas_call(
        _experts_kernel,
        grid_spec=grid_spec,
        out_shape=jax.ShapeDtypeStruct((max_tiles * tm, d), jnp.float32),
        compiler_params=_cparams(("arbitrary", "arbitrary")),
        name="experts",
    )(plan, meta, xs, wg_b, wu_b, wd_b)


def _combine_kernel(pos_ref, pos_next_ref, ys_hbm, h1_ref, g_ref, b_ref, *rest, alpha, final):
    if final:
        out_ref, ybuf_ref, sem = rest
        i = pl.program_id(0) * pl.num_programs(1) + pl.program_id(1)
        last = pl.num_programs(0) * pl.num_programs(1) - 1
    else:
        h_ref, hb_ref, ybuf_ref, sem = rest
        i = pl.program_id(0)
        last = pl.num_programs(0) - 1
    slot = i % 2
    tn = h1_ref.shape[0]

    def start_gather(idx_ref, s):
        _start_row_copies(idx_ref, tn, lambda g, k, hi, lo: pltpu.make_async_copy(
            ys_hbm.at[hi, pl.ds(lo, 1), :], ybuf_ref.at[s, g, pl.ds(k, 1), :], sem.at[s]))

    @pl.when(i == 0)
    def _():
        start_gather(pos_ref, 0)

    pltpu.make_async_copy(ys_hbm.at[pl.ds(0, tn // SUBLANES)], ybuf_ref.at[slot], sem.at[slot]).wait()

    @pl.when(i < last)
    def _():
        start_gather(pos_next_ref, 1 - slot)

    gathered = ybuf_ref[slot].reshape(tn, h1_ref.shape[1])
    y = _layer_norm_rows(alpha * h1_ref[...] + gathered, g_ref[0], b_ref[0])
    if final:
        out_ref[0] = y
    else:
        h_ref[...] = y
        hb_ref[...] = y.astype(jnp.bfloat16)


def _combine_final(pos, ys, h1, ln_g, ln_b, layer, alpha, batch, seq, tp):
    n, d = h1.shape
    tn = tp - seq
    per_batch = tp // tn
    nx = seq // tn
    pos3 = pos.reshape(batch * per_batch, 1, tn)
    wmap = lambda b, j: (layer, 0, 0)
    cur = lambda b, j: b * per_batch + j

    def nxt(b, j):
        wrap = j + 1 >= nx
        return jnp.where(wrap, jnp.minimum(b + 1, batch - 1) * per_batch, b * per_batch + j + 1)

    return pl.pallas_call(
        functools.partial(_combine_kernel, alpha=alpha, final=True),
        grid=(batch, nx),
        in_specs=[
            pl.BlockSpec((1, 1, tn), lambda b, j: (cur(b, j), 0, 0), memory_space=pltpu.SMEM),
            pl.BlockSpec((1, 1, tn), lambda b, j: (nxt(b, j), 0, 0), memory_space=pltpu.SMEM),
            pl.BlockSpec(memory_space=pl.ANY),
            pl.BlockSpec((tn, d), lambda b, j: (cur(b, j), 0)),
            _const_spec((1, 1, d), wmap),
            _const_spec((1, 1, d), wmap),
        ],
        out_specs=pl.BlockSpec((1, tn, d), lambda b, j: (b, j, 0)),
        out_shape=jax.ShapeDtypeStruct((batch, seq, d), jnp.float32),
        scratch_shapes=[pltpu.VMEM((2, tn // SUBLANES, SUBLANES, d), jnp.float32),
                        pltpu.SemaphoreType.DMA((2,))],
        compiler_params=_cparams(("arbitrary", "arbitrary")),
        name="combine_final",
    )(pos3, pos3, ys, h1, ln_g, ln_b)


def _combine(pos, ys, h1, ln_g, ln_b, layer, alpha, tn):
    n, d = h1.shape
    steps = n // tn
    pos3 = pos.reshape(steps, 1, tn)
    wmap = lambda i: (layer, 0, 0)
    rmap = lambda i: (i, 0)
    return pl.pallas_call(
        functools.partial(_combine_kernel, alpha=alpha, final=False),
        grid=(steps,),
        in_specs=[
            pl.BlockSpec((1, 1, tn), lambda i: (i, 0, 0), memory_space=pltpu.SMEM),
            pl.BlockSpec((1, 1, tn), lambda i: (jnp.minimum(i + 1, steps - 1), 0, 0), memory_space=pltpu.SMEM),
            pl.BlockSpec(memory_space=pl.ANY),
            pl.BlockSpec((tn, d), rmap),
            _const_spec((1, 1, d), wmap),
            _const_spec((1, 1, d), wmap),
        ],
        out_specs=[pl.BlockSpec((tn, d), rmap), pl.BlockSpec((tn, d), rmap)],
        out_shape=[jax.ShapeDtypeStruct((n, d), jnp.float32), jax.ShapeDtypeStruct((n, d), jnp.bfloat16)],
        scratch_shapes=[pltpu.VMEM((2, tn // SUBLANES, SUBLANES, d), jnp.float32),
                        pltpu.SemaphoreType.DMA((2,))],
        compiler_params=_cparams(("arbitrary",)),
        name="combine",
    )(pos3, pos3, ys, h1, ln_g, ln_b)


def _rope_tables(seq, tp):
    rows = seq // GRID_W
    pad = tp - seq - N_META
    pos_row = jnp.concatenate([jnp.repeat(jnp.arange(rows, dtype=jnp.float32), GRID_W),
                               jnp.full((N_META,), -1.0, jnp.float32), jnp.zeros((pad,), jnp.float32)])
    pos_col = jnp.concatenate([jnp.tile(jnp.arange(GRID_W, dtype=jnp.float32), rows),
                               jnp.arange(N_META, dtype=jnp.float32), jnp.zeros((pad,), jnp.float32)])

    def tables(rot_dim):
        n = rot_dim // 4
        inv = ROPE_THETA ** (-jnp.arange(n, dtype=jnp.float32) / n)
        ang = jnp.concatenate([pos_row[:, None] * inv, pos_col[:, None] * inv], axis=-1)
        cos, sin = jnp.cos(ang), jnp.sin(ang)
        reps = LANES // rot_dim
        return (jnp.tile(jnp.concatenate([cos, cos], axis=-1), (1, reps)),
                jnp.tile(jnp.concatenate([-sin, sin], axis=-1), (1, reps)))

    cos_a, sin_a = tables(ROPE_DIM)
    cos_b, sin_b = tables(GQA_HEAD_DIM)
    return cos_a, sin_a, cos_b, sin_b


def _relayout_weights(w_in, w_q_b, w_kv_b):
    bf16 = jnp.bfloat16
    off_kr = Q_LORA + KV_LORA
    off_gq = off_kr + ROPE_DIM
    kr = w_in[:, :, off_kr:off_gq]
    w_in_p = jnp.concatenate([w_in[:, :, :off_kr], kr, kr, w_in[:, :, off_gq:]], axis=-1).astype(bf16)
    depth = w_in.shape[0]
    wq = w_q_b.reshape(depth, Q_LORA, MLA_HEADS, QK_HEAD)
    wq_p = jnp.concatenate([wq[..., :NOPE_DIM].reshape(depth, Q_LORA, -1),
                            wq[..., NOPE_DIM:].reshape(depth, Q_LORA, -1)], axis=-1).astype(bf16)
    wkv = w_kv_b.reshape(depth, KV_LORA, MLA_HEADS, NOPE_DIM + V_DIM)
    wkv_p = jnp.concatenate([wkv[..., :NOPE_DIM].reshape(depth, KV_LORA, -1),
                             wkv[..., NOPE_DIM:].reshape(depth, KV_LORA, -1)], axis=-1).astype(bf16)
    return w_in_p, wq_p, wkv_p


def _dispatch_plan(ri, counts, max_tiles, tp, n_tok):
    tm = EXPERT_TILE
    n = ri.shape[0]
    rows = max_tiles * tm
    cls, rank = ri[:, 0], ri[:, 1]
    cnt = counts[0, :N_CLASSES].astype(jnp.int32)
    tiles_c = (cnt + tm - 1) // tm
    tile_end = jnp.cumsum(tiles_c)
    tile_start = tile_end - tiles_c
    n_tiles = tile_end[-1]
    routed = cls >= 0
    tok = jnp.arange(n, dtype=jnp.int32)
    spare = rows + (tok // tp) * (tp - n_tok) + (tok % tp - n_tok)
    sorted_pos = tile_start[jnp.maximum(cls, 0)] * tm + rank
    pos_scatter = jnp.where(routed, sorted_pos, spare).astype(jnp.int32)
    pos_gather = jnp.where(routed, sorted_pos, 0).astype(jnp.int32)

    t = jnp.minimum(jnp.arange(max_tiles, dtype=jnp.int32), n_tiles - 1)
    tile_cls = jnp.sum((t[:, None] >= tile_end[None, :]).astype(jnp.int32), axis=1)
    tile_cls = jnp.minimum(tile_cls, N_CLASSES - 1)
    pair_lo = jnp.array([0, 0, 0, 1, 1, 2], jnp.int32)
    pair_hi = jnp.array([1, 2, 3, 2, 3, 3], jnp.int32)
    e_a = EXPERTS_PER_GROUP * (tile_cls // PAIRS_PER_GROUP) + pair_lo[tile_cls % PAIRS_PER_GROUP]
    e_b = EXPERTS_PER_GROUP * (tile_cls // PAIRS_PER_GROUP) + pair_hi[tile_cls % PAIRS_PER_GROUP]
    odd = (jnp.arange(max_tiles) % 2) == 1
    tile_e = jnp.stack([jnp.where(odd, e_b, e_a), jnp.where(odd, e_a, e_b)], axis=-1).reshape(-1)
    step_tile = jnp.arange(2 * max_tiles) // 2
    tile_e = jnp.where(step_tile >= n_tiles, tile_e[2 * n_tiles - 1], tile_e).astype(jnp.int32)
    n_steps = 2 * max_tiles
    step = jnp.arange(n_steps, dtype=jnp.int32)
    used = step_tile < n_tiles
    prev_e = jnp.concatenate([jnp.full((1,), -1, jnp.int32), tile_e[:-1]])
    run_start = used & (tile_e != prev_e)
    slot = jnp.cumsum(run_start.astype(jnp.int32)) % 2
    later_start = run_start[None, :] & (step[None, :] > step[:, None])
    next_start = jnp.min(jnp.where(later_start, step[None, :], n_steps), axis=1)
    next_e = jnp.where(next_start < n_steps, tile_e[jnp.minimum(next_start, n_steps - 1)], -1)
    plan = jnp.concatenate([tile_e, run_start.astype(jnp.int32), slot, next_e]).astype(jnp.int32)
    meta = jnp.stack([n_tiles, n_tiles]).astype(jnp.int32)
    return plan, meta, pos_scatter, pos_gather


def kernel(x, meta_tokens, ln_in_g, ln_in_b, w_in, g_q_lora, w_q_b, g_kv_lora, w_kv_b, g_qk_q, g_qk_k,
           g_out_mla, g_out_gqa, w_out, ln1_g, ln1_b, w_router, router_bias, w_gate, w_up, w_down,
           ln2_g, ln2_b):
    batch, seq, d = x.shape
    depth = w_in.shape[0]
    n_tok = seq + N_META
    tp = -(-n_tok // LANES) * LANES
    n = batch * tp
    alpha = (2.0 * depth) ** 0.25
    bf16 = jnp.bfloat16
    f32 = jnp.float32

    tabs = _rope_tables(seq, tp)
    w_in_p, wq_p, wkv_p = _relayout_weights(w_in, w_q_b, w_kv_b)
    w_out_b = w_out.astype(bf16)
    n_exp, ff = w_gate.shape[1], w_gate.shape[3]
    wg_rows = w_gate.reshape(depth * n_exp * d, ff)
    wu_rows = w_up.reshape(depth * n_exp * d, ff)
    wd_rows = w_down.reshape(depth * n_exp * ff, d)
    wr = jnp.pad(w_router.astype(f32), ((0, 0), (0, LANES - N_EXPERTS)))
    wr_hi = wr.astype(bf16)
    wr_lo = (wr - wr_hi.astype(f32)).astype(bf16)
    wr_hl = jnp.concatenate([wr_hi, wr_lo], axis=1)
    rbias = jnp.pad(router_bias.astype(f32), (0, LANES - N_EXPERTS)).reshape(1, LANES)
    row3 = lambda a: a.reshape(depth, 1, a.shape[-1])

    h, hb = _ln_in(x, meta_tokens.astype(x.dtype), ln_in_g.reshape(1, d), ln_in_b.reshape(1, d), tp)

    tm_e = EXPERT_TILE
    max_tiles = -(-(batch * n_tok) // tm_e) + N_CLASSES
    spare_rows = -(-(batch * (tp - n_tok)) // tm_e) * tm_e
    tn = _row_tile(n, 256)
    xs_rows, row_w = max_tiles * tm_e + spare_rows, d // 2 + LANES
    xs = jnp.zeros((xs_rows // SUBLANES, SUBLANES, row_w), jnp.int32)
    for l in range(depth):
        q_a, k_a, v_a, q_b, k_b, v_b = _in_proj(
            hb, w_in_p, wq_p, wkv_p, row3(g_q_lora), row3(g_kv_lora), row3(g_qk_q), row3(g_qk_k),
            tabs, l, batch, tp)
        o_a, (wg_b, wu_b) = _attention(q_a, k_a, v_a, n_tok, [(wg_rows, (l, depth)), (wu_rows, (l, depth))])
        o_b, (wd_b,) = _attention(q_b, k_b, v_b, n_tok, [(wd_rows, (l, depth))])
        wg_b, wu_b = wg_b.reshape(n_exp, d, ff), wu_b.reshape(n_exp, d, ff)
        wd_b = wd_b.reshape(n_exp, ff, d)
        h1, h1p, ri, counts = _out_proj(
            o_a, o_b, h, w_out_b, row3(g_out_mla), row3(g_out_gqa), row3(ln1_g), row3(ln1_b),
            wr_hl, rbias, l, alpha, tp, n_tok)
        tile_e, tmeta, pos_scatter, pos_gather = _dispatch_plan(ri, counts, max_tiles, tp, n_tok)
        xs = _dispatch(pos_scatter.reshape(n // tn, 1, tn), h1p, xs)
        ys = _experts(tile_e, tmeta, xs.reshape(xs_rows, row_w), wg_b, wu_b, wd_b, max_tiles)
        ys = ys.reshape(ys.shape[0] // SUBLANES, SUBLANES, d)
        if l + 1 < depth:
            h, hb = _combine(pos_gather, ys, h1, row3(ln2_g), row3(ln2_b), l, alpha, tn)
        else:
            out = _combine_final(pos_gather, ys, h1, row3(ln2_g), row3(ln2_b), l, alpha, batch, seq, tp)
    return out
```

```python
import functools
import math

import jax
import jax.numpy as jnp
import numpy as np
from jax import lax
from jax.experimental import pallas as pl
from jax.experimental.pallas import tpu as pltpu

N_META = 16
GRID_W = 64
ROPE_THETA = 10000.0
EPS = 1e-6

MLA_HEADS = 8
Q_LORA = 512
KV_LORA = 256
NOPE_DIM = 128
ROPE_DIM = 64
V_DIM = 128
QK_HEAD = NOPE_DIM + ROPE_DIM
MLA_WIDTH = MLA_HEADS * V_DIM
LOG2_E = math.log2(math.e)
MLA_SCALE = LOG2_E / math.sqrt(QK_HEAD)
MLA_QK_PAD = 256

GQA_HEADS = 8
GQA_KV_HEADS = 2
GQA_HEAD_DIM = 128
GQA_WIDTH = GQA_HEADS * GQA_HEAD_DIM
GQA_SCALE = LOG2_E / math.sqrt(GQA_HEAD_DIM)

N_EXPERTS = 16
N_GROUPS = 4
EXPERTS_PER_GROUP = 4
PAIRS_PER_GROUP = 6
N_CLASSES = N_GROUPS * PAIRS_PER_GROUP
EXPERT_FF = 1024

LANES = 128
VMEM_LIMIT_BYTES = 58 * 1024 * 1024
EXPERT_TILE = 256
NEG_BIG = -1e30
HI16 = -65536

C_CQ = 0
C_CKV = C_CQ + Q_LORA
C_KR = C_CKV + KV_LORA
C_GQ = C_KR + LANES
C_GK = C_GQ + GQA_WIDTH
C_GV = C_GK + GQA_KV_HEADS * GQA_HEAD_DIM
IN_COLS_PAD = C_GV + GQA_KV_HEADS * GQA_HEAD_DIM


def _cparams(semantics, flags=None):
    return pltpu.CompilerParams(dimension_semantics=semantics, vmem_limit_bytes=VMEM_LIMIT_BYTES, flags=flags)


def _const_spec(block_shape, index_map):
    return pl.BlockSpec(block_shape, index_map, pipeline_mode=pl.Buffered(1))


def _row_tile(n, cap, mult=16):
    best = mult
    for t in range(mult, cap + 1, mult):
        if n % t == 0:
            best = t
    assert n % best == 0
    return best


def _layer_norm_rows(z, g, b):
    mu = jnp.mean(z, axis=-1, keepdims=True)
    zc = z - mu
    var = jnp.mean(zc * zc, axis=-1, keepdims=True)
    return zc * lax.rsqrt(var + EPS) * g + b


def _rms_rows(z, g):
    return z * lax.rsqrt(jnp.mean(z * z, axis=-1, keepdims=True) + EPS) * g


def _dot(a, b):
    return jnp.dot(a, b, preferred_element_type=jnp.float32)


def _pack_bf16_pair(lo_f32, hi_f32):
    lo_bits = lax.bitcast_convert_type(lo_f32.astype(jnp.bfloat16).astype(jnp.float32), jnp.int32)
    hi_bits = lax.bitcast_convert_type(hi_f32.astype(jnp.bfloat16).astype(jnp.float32), jnp.int32)
    return lax.shift_right_logical(lo_bits, 16) | (hi_bits & HI16)


def _unpack_bf16_pair(words, dtype=jnp.bfloat16):
    lo = lax.bitcast_convert_type(words << 16, jnp.float32).astype(dtype)
    hi = lax.bitcast_convert_type(words & HI16, jnp.float32).astype(dtype)
    return lo, hi


def _ln_in_kernel(x_ref, meta_ref, g_ref, b_ref, h_ref, hb_ref):
    j = pl.program_id(1)
    last = pl.num_programs(1) - 1

    def emit(rows):
        y = _layer_norm_rows(rows, g_ref[...], b_ref[...])
        h_ref[...] = y
        hb_ref[...] = y.astype(jnp.bfloat16)

    @pl.when(j < last)
    def _():
        emit(x_ref[0])

    @pl.when(j == last)
    def _():
        meta = meta_ref[...]
        pad = jnp.zeros((h_ref.shape[0] - meta.shape[0], meta.shape[1]), meta.dtype)
        emit(jnp.concatenate([meta, pad], axis=0))


def _ln_in(x, meta_tokens, g, b, tp):
    batch, seq, d = x.shape
    tm = tp - seq
    assert seq % tm == 0 and meta_tokens.shape[0] <= tm
    nt = tp // tm
    n = batch * tp
    omap = lambda bi, j: (bi * nt + j, 0)
    cmap = lambda bi, j: (0, 0)
    return pl.pallas_call(
        _ln_in_kernel,
        grid=(batch, nt),
        in_specs=[
            pl.BlockSpec((1, tm, d), lambda bi, j: (bi, jnp.minimum(j, nt - 2), 0)),
            pl.BlockSpec(meta_tokens.shape, cmap),
            pl.BlockSpec((1, d), cmap),
            pl.BlockSpec((1, d), cmap),
        ],
        out_specs=[pl.BlockSpec((tm, d), omap), pl.BlockSpec((tm, d), omap)],
        out_shape=[jax.ShapeDtypeStruct((n, d), jnp.float32), jax.ShapeDtypeStruct((n, d), jnp.bfloat16)],
        compiler_params=_cparams(("parallel", "parallel")),
        name="ln_in",
    )(x, meta_tokens, g, b)


def _swap_halves_64(x):
    lane = lax.broadcasted_iota(jnp.int32, x.shape, 1)
    fwd = pltpu.roll(x, 32, 1)
    bwd = pltpu.roll(x, 96, 1)
    return jnp.where((lane & 63) < 32, bwd, fwd)


def _in_proj_kernel(x_ref, w_in_ref, wq_ref, wkv_ref, gq_ref, gkv_ref, gqq_ref, gqk_ref,
                    cos_a_ref, sin_a_ref, cos_b_ref, sin_b_ref,
                    q_mla_ref, k_mla_ref, v_mla_ref, q_gqa_ref, k_gqa_ref, v_gqa_ref):
    bf16 = jnp.bfloat16
    u = _dot(x_ref[...], w_in_ref[0])
    cos_a, sin_a = cos_a_ref[...], sin_a_ref[...]
    cos_b, sin_b = cos_b_ref[...], sin_b_ref[...]
    lane = lax.broadcasted_iota(jnp.int32, cos_a.shape, 1)

    c_q = _rms_rows(u[:, C_CQ:C_CKV], gq_ref[0]).astype(bf16)
    qa = _dot(c_q, wq_ref[0])
    c_kv = _rms_rows(u[:, C_CKV:C_KR], gkv_ref[0]).astype(bf16)
    kv = _dot(c_kv, wkv_ref[0])
    kr = u[:, C_KR:C_GQ]
    kr = kr * cos_a + _swap_halves_64(kr) * sin_a
    rope_base = MLA_HEADS * NOPE_DIM
    for pair in range(MLA_HEADS // 2):
        blk = qa[:, rope_base + LANES * pair:rope_base + LANES * (pair + 1)]
        rot = blk * cos_a + _swap_halves_64(blk) * sin_a
        for half in range(2):
            h = 2 * pair + half
            own = jnp.where((lane >> 6) == half, rot, 0.0)
            qh = jnp.concatenate([qa[:, NOPE_DIM * h:NOPE_DIM * (h + 1)], own], axis=1) * MLA_SCALE
            q_mla_ref[0, h] = qh.astype(bf16)
    for h in range(MLA_HEADS):
        kh = jnp.concatenate([kv[:, NOPE_DIM * h:NOPE_DIM * (h + 1)], kr], axis=1)
        k_mla_ref[0, h] = kh.astype(bf16)
        v0 = MLA_HEADS * NOPE_DIM + V_DIM * h
        v_mla_ref[0, h] = kv[:, v0:v0 + V_DIM].astype(bf16)

    for h in range(GQA_HEADS):
        qh = _rms_rows(u[:, C_GQ + GQA_HEAD_DIM * h:C_GQ + GQA_HEAD_DIM * (h + 1)], gqq_ref[0])
        qh = qh * cos_b + pltpu.roll(qh, 64, 1) * sin_b
        q_gqa_ref[0, h] = (qh * GQA_SCALE).astype(bf16)
    for h in range(GQA_KV_HEADS):
        kh = _rms_rows(u[:, C_GK + GQA_HEAD_DIM * h:C_GK + GQA_HEAD_DIM * (h + 1)], gqk_ref[0])
        kh = kh * cos_b + pltpu.roll(kh, 64, 1) * sin_b
        k_gqa_ref[0, h] = kh.astype(bf16)
        v_gqa_ref[0, h] = u[:, C_GV + GQA_HEAD_DIM * h:C_GV + GQA_HEAD_DIM * (h + 1)].astype(bf16)


def _in_proj(hb, w_in_p, wq_p, wkv_p, g_q, g_kv, g_qq, g_qk, tabs, layer, batch, tp):
    n, d = hb.shape
    tm = _row_tile(tp, 320)
    nt = tp // tm
    cos_a, sin_a, cos_b, sin_b = tabs
    wmap = lambda b, i: (layer, 0, 0)
    tmap = lambda b, i: (i, 0)
    omap = lambda b, i: (b, 0, i, 0)
    bf16 = jnp.bfloat16
    return pl.pallas_call(
        _in_proj_kernel,
        grid=(batch, nt),
        in_specs=[
            pl.BlockSpec((tm, d), lambda b, i: (b * nt + i, 0)),
            _const_spec((1, d, IN_COLS_PAD), wmap),
            _const_spec((1, Q_LORA, wq_p.shape[2]), wmap),
            _const_spec((1, KV_LORA, wkv_p.shape[2]), wmap),
            _const_spec((1, 1, Q_LORA), wmap),
            _const_spec((1, 1, KV_LORA), wmap),
            _const_spec((1, 1, GQA_HEAD_DIM), wmap),
            _const_spec((1, 1, GQA_HEAD_DIM), wmap),
            pl.BlockSpec((tm, LANES), tmap),
            pl.BlockSpec((tm, LANES), tmap),
            pl.BlockSpec((tm, LANES), tmap),
            pl.BlockSpec((tm, LANES), tmap),
        ],
        out_specs=[
            pl.BlockSpec((1, MLA_HEADS, tm, MLA_QK_PAD), omap),
            pl.BlockSpec((1, MLA_HEADS, tm, MLA_QK_PAD), omap),
            pl.BlockSpec((1, MLA_HEADS, tm, V_DIM), omap),
            pl.BlockSpec((1, GQA_HEADS, tm, GQA_HEAD_DIM), omap),
            pl.BlockSpec((1, GQA_KV_HEADS, tm, GQA_HEAD_DIM), omap),
            pl.BlockSpec((1, GQA_KV_HEADS, tm, GQA_HEAD_DIM), omap),
        ],
        out_shape=[
            jax.ShapeDtypeStruct((batch, MLA_HEADS, tp, MLA_QK_PAD), bf16),
            jax.ShapeDtypeStruct((batch, MLA_HEADS, tp, MLA_QK_PAD), bf16),
            jax.ShapeDtypeStruct((batch, MLA_HEADS, tp, V_DIM), bf16),
            jax.ShapeDtypeStruct((batch, GQA_HEADS, tp, GQA_HEAD_DIM), bf16),
            jax.ShapeDtypeStruct((batch, GQA_KV_HEADS, tp, GQA_HEAD_DIM), bf16),
            jax.ShapeDtypeStruct((batch, GQA_KV_HEADS, tp, GQA_HEAD_DIM), bf16),
        ],
        compiler_params=_cparams(("parallel", "parallel")),
        name="in_proj",
    )(hb, w_in_p, wq_p, wkv_p, g_q, g_kv, g_qq, g_qk, cos_a, sin_a, cos_b, sin_b)


def _attn_kernel(q_ref, k_ref, v_ref, *rest, tq, n_keys, n_cast):
    o_ref = rest[n_cast]
    for src_ref, dst_ref in zip(rest[:n_cast], rest[n_cast + 1:]):
        dst_ref[...] = src_ref[...].astype(dst_ref.dtype)
    tp = k_ref.shape[2]
    body = tp - LANES
    nt = (((1,), (1,)), ((), ()))
    k_body, k_tail = k_ref[0, 0, :body, :], k_ref[0, 0, body:, :]
    v_body, v_tail = v_ref[0, 0, :body, :], v_ref[0, 0, body:, :]
    tail_ok = lax.broadcasted_iota(jnp.int32, (1, LANES), 1) < (n_keys - body)
    n_q = -(-n_keys // 16) * 16
    if n_q < tp:
        o_ref[n_q:, :] = jnp.zeros((tp - n_q, o_ref.shape[1]), o_ref.dtype)
    for r0 in range(0, n_q, tq):
        rows = slice(r0, min(r0 + tq, n_q))
        q = q_ref[0, 0, rows, :]
        s_body = lax.dot_general(q, k_body, nt, preferred_element_type=jnp.float32)
        s_tail = lax.dot_general(q, k_tail, nt, preferred_element_type=jnp.float32)
        s_tail = jnp.where(tail_ok, s_tail, NEG_BIG)
        m = jnp.maximum(jnp.max(s_body, axis=-1, keepdims=True), jnp.max(s_tail, axis=-1, keepdims=True))
        p_body = jnp.exp2(s_body - m)
        p_tail = jnp.exp2(s_tail - m)
        l = jnp.sum(p_body, axis=-1, keepdims=True) + jnp.sum(p_tail, axis=-1, keepdims=True)
        o = _dot(p_body.astype(jnp.bfloat16), v_body) + _dot(p_tail.astype(jnp.bfloat16), v_tail)
        o_ref[rows, :] = (o / l).astype(o_ref.dtype)


def _attention(q, k, v, n_keys, casts):
    batch, hq, tp, dk = q.shape
    hk, dv = k.shape[1], v.shape[3]
    rep = hq // hk
    tq = _row_tile(tp, 320)
    steps = batch * hq
    cast_in, cast_out, cast_shapes = [], [], []
    for w, layer in casts:
        rows = w.shape[0] // (layer[1])
        assert rows % steps == 0
        blk = rows // steps
        off = layer[0] * steps
        cast_in.append(pl.BlockSpec((blk, w.shape[1]), lambda b, h, off=off: (off + b * hq + h, 0)))
        cast_out.append(pl.BlockSpec((blk, w.shape[1]), lambda b, h: (b * hq + h, 0)))
        cast_shapes.append(jax.ShapeDtypeStruct((rows, w.shape[1]), jnp.bfloat16))
    outs = pl.pallas_call(
        functools.partial(_attn_kernel, tq=tq, n_keys=n_keys, n_cast=len(casts)),
        grid=(batch, hq),
        in_specs=[
            pl.BlockSpec((1, 1, tp, dk), lambda b, h: (b, h, 0, 0)),
            pl.BlockSpec((1, 1, tp, dk), lambda b, h: (b, h // rep, 0, 0)),
            pl.BlockSpec((1, 1, tp, dv), lambda b, h: (b, h // rep, 0, 0)),
        ] + cast_in,
        out_specs=[pl.BlockSpec((tp, dv), lambda b, h: (b, h))] + cast_out,
        out_shape=[jax.ShapeDtypeStruct((batch * tp, hq * dv), jnp.bfloat16)] + cast_shapes,
        compiler_params=_cparams(("parallel", "parallel")),
        name="attention",
    )(q, k, v, *[w for w, _ in casts])
    return outs[0], outs[1:]


def _first_index_of_max(vals, lane_f):
    m = jnp.max(vals, axis=-1, keepdims=True)
    idx = jnp.min(jnp.where(vals == m, lane_f, float(LANES)), axis=-1, keepdims=True)
    return m, idx


def _out_proj_kernel(oa_ref, ob_ref, h_ref, w_out_ref, ga_ref, gb_ref, lg_ref, lb_ref,
                     wr_hl_ref, rbias_ref,
                     h1_ref, h1p_ref, ri_ref, cnt_ref, carry_ref, *, alpha, tp, n_tok, sub):
    i = pl.program_id(0)
    tm = h_ref.shape[0]

    @pl.when(i == 0)
    def _():
        carry_ref[...] = jnp.zeros_like(carry_ref)

    blocks = [slice(r0, r0 + sub) for r0 in range(0, tm, sub)]
    mixed = [_out_proj_matmul(rows, oa_ref, ob_ref, w_out_ref, ga_ref, gb_ref) for rows in blocks]
    routed_in = [_out_proj_norm(rows, m, h_ref, lg_ref, lb_ref, wr_hl_ref, h1_ref, alpha=alpha)
                 for rows, m in zip(blocks, mixed)]
    for rows, (h1, logits) in zip(blocks, routed_in):
        _out_proj_route(rows, i * tm + rows.start, h1, logits, rbias_ref, h1p_ref, ri_ref, carry_ref,
                        tp=tp, n_tok=n_tok)
    cnt_ref[...] = carry_ref[...]


def _out_proj_matmul(rows, oa_ref, ob_ref, w_out_ref, ga_ref, gb_ref):
    bf16 = jnp.bfloat16
    na = _rms_rows(oa_ref[rows, :].astype(jnp.float32), ga_ref[0]).astype(bf16)
    nb = _rms_rows(ob_ref[rows, :].astype(jnp.float32), gb_ref[0]).astype(bf16)
    return _dot(na, w_out_ref[0, :MLA_WIDTH, :]) + _dot(nb, w_out_ref[0, MLA_WIDTH:, :])


def _out_proj_norm(rows, mixed, h_ref, lg_ref, lb_ref, wr_hl_ref, h1_ref, *, alpha):
    bf16 = jnp.bfloat16
    h1 = _layer_norm_rows(alpha * h_ref[rows, :] + mixed, lg_ref[0], lb_ref[0])
    h1_ref[rows, :] = h1
    hi = h1.astype(bf16)
    lo = (h1 - hi.astype(jnp.float32)).astype(bf16)
    hi_prod = _dot(hi, wr_hl_ref[...])
    logits = hi_prod[:, :LANES] + hi_prod[:, LANES:] + _dot(lo, wr_hl_ref[:, :LANES])
    return h1, logits


def _out_proj_route(rows, row0, h1, logits, rbias_ref, h1p_ref, ri_ref, carry_ref, *, tp, n_tok):
    bf16 = jnp.bfloat16
    f32 = jnp.float32
    tm = rows.stop - rows.start
    half = h1.shape[1] // 2
    scores = jax.nn.sigmoid(logits)

    lane = lax.broadcasted_iota(jnp.int32, (tm, LANES), 1)
    lane_f = lane.astype(f32)
    neg = -jnp.inf
    sel = jnp.where(lane < N_EXPERTS, scores + rbias_ref[...], neg)
    grp = lane >> 2

    best = None
    for g in range(N_GROUPS):
        mg = jnp.where(grp == g, sel, neg)
        m1, i1 = _first_index_of_max(mg, lane_f)
        m2 = jnp.max(jnp.where(lane_f == i1, neg, mg), axis=-1, keepdims=True)
        gs = m1 + m2
        if best is None:
            best, gi = gs, jnp.zeros_like(gs)
        else:
            better = gs > best
            gi = jnp.where(better, float(g), gi)
            best = jnp.where(better, gs, best)

    mg = jnp.where(grp.astype(f32) == gi, sel, neg)
    _, e1 = _first_index_of_max(mg, lane_f)
    mg2 = jnp.where(lane_f == e1, neg, mg)
    _, e2 = _first_index_of_max(mg2, lane_f)
    w1 = jnp.sum(jnp.where(lane_f == e1, scores, 0.0), axis=-1, keepdims=True)
    w2 = jnp.sum(jnp.where(lane_f == e2, scores, 0.0), axis=-1, keepdims=True)
    den = w1 + w2
    w1, w2 = w1 / den, w2 / den

    first_lower = e1 < e2
    la = jnp.where(first_lower, e1, e2) - EXPERTS_PER_GROUP * gi
    lb = jnp.where(first_lower, e2, e1) - EXPERTS_PER_GROUP * gi
    w_a = jnp.where(first_lower, w1, w2)
    w_b = jnp.where(first_lower, w2, w1)
    cls = PAIRS_PER_GROUP * gi + la * (7.0 - la) * 0.5 + (lb - la - 1.0)

    row = (row0 + lax.broadcasted_iota(jnp.int32, (tm, 1), 0)).astype(f32)
    routed = (row - jnp.floor((row + 0.5) * (1.0 / tp)) * tp) < n_tok
    onehot = jnp.where((lane_f == cls) & routed, 1.0, 0.0)

    r_i = lax.broadcasted_iota(jnp.int32, (tm, tm), 0)
    c_i = lax.broadcasted_iota(jnp.int32, (tm, tm), 1)
    lower = jnp.where(c_i < r_i, 1.0, 0.0).astype(bf16)
    before = _dot(lower, onehot.astype(bf16)) + carry_ref[...]
    rank = jnp.sum(onehot * before, axis=-1, keepdims=True)
    carry_ref[...] += jnp.sum(onehot, axis=0, keepdims=True)

    cls_out = jnp.where(routed, cls, -1.0)
    ri_ref[rows, :] = jnp.where(lane == 0, cls_out, jnp.where(lane == 1, rank, 0.0)).astype(jnp.int32)

    h1p_ref[rows, :half] = _pack_bf16_pair(h1[:, :half], h1[:, half:])
    weights = jnp.where(lane == 0, w_a, jnp.where(lane == 1, w_b, 0.0))
    h1p_ref[rows, half:] = lax.bitcast_convert_type(weights, jnp.int32)


def _out_proj(o_a, o_b, h, w_out_b, g_a, g_b, ln_g, ln_b, wr_hl, rbias, layer, alpha, tp, n_tok):
    n, d = h.shape
    sub = _row_tile(n, 256)
    tm = 2 * sub if n % (2 * sub) == 0 else sub
    row_w = d // 2 + LANES
    wmap = lambda i: (layer, 0, 0)
    cmap = lambda i: (0, 0)
    rmap = lambda i: (i, 0)
    return pl.pallas_call(
        functools.partial(_out_proj_kernel, alpha=alpha, tp=tp, n_tok=n_tok, sub=sub),
        grid=(n // tm,),
        in_specs=[
            pl.BlockSpec((tm, MLA_WIDTH), rmap),
            pl.BlockSpec((tm, GQA_WIDTH), rmap),
            pl.BlockSpec((tm, d), rmap),
            _const_spec((1, MLA_WIDTH + GQA_WIDTH, d), wmap),
            _const_spec((1, 1, MLA_WIDTH), wmap),
            _const_spec((1, 1, GQA_WIDTH), wmap),
            _const_spec((1, 1, d), wmap),
            _const_spec((1, 1, d), wmap),
            _const_spec((d, 2 * LANES), cmap),
            _const_spec((1, LANES), cmap),
        ],
        out_specs=[
            pl.BlockSpec((tm, d), rmap),
            pl.BlockSpec((tm, row_w), rmap),
            pl.BlockSpec((tm, LANES), rmap),
            pl.BlockSpec((1, LANES), cmap),
        ],
        out_shape=[
            jax.ShapeDtypeStruct((n, d), jnp.float32),
            jax.ShapeDtypeStruct((n, row_w), jnp.int32),
            jax.ShapeDtypeStruct((n, LANES), jnp.int32),
            jax.ShapeDtypeStruct((1, LANES), jnp.float32),
        ],
        scratch_shapes=[pltpu.VMEM((1, LANES), jnp.float32)],
        compiler_params=_cparams(("arbitrary",)),
        name="out_proj",
    )(o_a, o_b, h, w_out_b, g_a, g_b, ln_g, ln_b, wr_hl, rbias)


SUBLANES = 8


def _start_row_copies(idx_ref, n_rows, make_copy):
    def issue(g, carry):
        for k in range(SUBLANES):
            idx = idx_ref[0, 0, g * SUBLANES + k]
            make_copy(g, k, lax.shift_right_logical(idx, 3), idx & (SUBLANES - 1)).start()
        return carry

    lax.fori_loop(0, n_rows // SUBLANES, issue, 0)


def _dispatch_kernel(pos_ref, x_ref, xs_init_hbm, xs_hbm, buf_ref, sem):
    del xs_init_hbm
    i = pl.program_id(0)
    last = pl.num_programs(0) - 1
    slot = i % 2
    tn = x_ref.shape[0]
    groups = tn // SUBLANES

    def wait_slot(s):
        pltpu.make_async_copy(buf_ref.at[s], xs_hbm.at[pl.ds(0, groups)], sem.at[s]).wait()

    @pl.when(i >= 2)
    def _():
        wait_slot(slot)

    buf_ref[slot] = x_ref[...].reshape(groups, SUBLANES, x_ref.shape[1])
    _start_row_copies(pos_ref, tn, lambda g, k, hi, lo: pltpu.make_async_copy(
        buf_ref.at[slot, g, pl.ds(k, 1), :], xs_hbm.at[hi, pl.ds(lo, 1), :], sem.at[slot]))

    @pl.when(i == last)
    def _():
        wait_slot(slot)

        @pl.when(last >= 1)
        def _():
            wait_slot(1 - slot)


def _dispatch(pos3, h1p, xs_init):
    n, row_w = h1p.shape
    tn = pos3.shape[2]
    return pl.pallas_call(
        _dispatch_kernel,
        grid=(n // tn,),
        in_specs=[
            pl.BlockSpec((1, 1, tn), lambda i: (i, 0, 0), memory_space=pltpu.SMEM),
            pl.BlockSpec((tn, row_w), lambda i: (i, 0)),
            pl.BlockSpec(memory_space=pl.ANY),
        ],
        out_specs=pl.BlockSpec(memory_space=pl.ANY),
        out_shape=jax.ShapeDtypeStruct(xs_init.shape, xs_init.dtype),
        scratch_shapes=[pltpu.VMEM((2, tn // SUBLANES, SUBLANES, row_w), h1p.dtype),
                        pltpu.SemaphoreType.DMA((2,))],
        input_output_aliases={2: 0},
        compiler_params=_cparams(("arbitrary",)),
        name="dispatch",
    )(pos3, h1p, xs_init)


def _experts_kernel(plan_ref, meta_ref, xs_ref, wg_hbm, wu_hbm, wd_hbm, ys_ref, acc_ref, wg_buf, wu_buf, wd_buf,
                    sem):
    i = pl.program_id(0)
    k = pl.program_id(1)
    steps = 2 * pl.num_programs(0)
    t = 2 * i + k
    half = wg_buf.shape[1] // 2

    def weight_copies(e, s):
        return (pltpu.make_async_copy(wg_hbm.at[e], wg_buf.at[s], sem.at[s]),
                pltpu.make_async_copy(wu_hbm.at[e], wu_buf.at[s], sem.at[s]),
                pltpu.make_async_copy(wd_hbm.at[e], wd_buf.at[s], sem.at[s]))

    @pl.when(i < meta_ref[0])
    def _():
        slot = plan_ref[2 * steps + t]

        @pl.when(plan_ref[steps + t] == 1)
        def _():
            @pl.when(t == 0)
            def _():
                for c in weight_copies(plan_ref[0], slot):
                    c.start()

            for c in weight_copies(plan_ref[t], slot):
                c.wait()
            nxt = plan_ref[3 * steps + t]

            @pl.when(nxt >= 0)
            def _():
                for c in weight_copies(nxt, 1 - slot):
                    c.start()

        lo, hi = _unpack_bf16_pair(xs_ref[:, :half])
        g = _dot(lo, wg_buf[slot, :half, :]) + _dot(hi, wg_buf[slot, half:, :])
        u = _dot(lo, wu_buf[slot, :half, :]) + _dot(hi, wu_buf[slot, half:, :])
        a = (g * jax.nn.sigmoid(g) * u).astype(jnp.bfloat16)
        y = _dot(a, wd_buf[slot])
        second = (k + i) % 2 == 1
        wab = lax.bitcast_convert_type(xs_ref[:, half:], jnp.float32)
        y = y * jnp.where(second, wab[:, 1:2], wab[:, 0:1])

        @pl.when(k == 0)
        def _():
            acc_ref[...] = y

        @pl.when(k == 1)
        def _():
            total = acc_ref[...] + y
            ys_ref[...] = _pack_bf16_pair(total[:, :half], total[:, half:])

    @pl.when((i >= meta_ref[0]) & (k == 1))
    def _():
        ys_ref[...] = jnp.zeros_like(ys_ref)


def _experts(plan, meta, xs, wg_b, wu_b, wd_b, max_tiles):
    tm = EXPERT_TILE
    row_w = xs.shape[1]
    d, ff = wg_b.shape[1], wg_b.shape[2]

    def xmap(i, k, pr, mt):
        return (jnp.minimum(i, mt[0] - 1), 0)

    grid_spec = pltpu.PrefetchScalarGridSpec(
        num_scalar_prefetch=2,
        grid=(max_tiles, 2),
        in_specs=[
            pl.BlockSpec((tm, row_w), xmap),
            pl.BlockSpec(memory_space=pl.ANY),
            pl.BlockSpec(memory_space=pl.ANY),
            pl.BlockSpec(memory_space=pl.ANY),
        ],
        out_specs=pl.BlockSpec((tm, d // 2), lambda i, k, pr, mt: (i, 0)),
        scratch_shapes=[
            pltpu.VMEM((tm, d), jnp.float32),
            pltpu.VMEM((2, d, ff), wg_b.dtype),
            pltpu.VMEM((2, d, ff), wu_b.dtype),
            pltpu.VMEM((2, ff, d), wd_b.dtype),
            pltpu.SemaphoreType.DMA((2,)),
        ],
    )
    return pl.pallas_call(
        _experts_kernel,
        grid_spec=grid_spec,
        out_shape=jax.ShapeDtypeStruct((max_tiles * tm, d // 2), jnp.int32),
        compiler_params=_cparams(("arbitrary", "arbitrary")),
        name="experts",
    )(plan, meta, xs, wg_b, wu_b, wd_b)


def _combine_kernel(pos_ref, pos_next_ref, ys_hbm, h1_ref, g_ref, b_ref, *rest, alpha, final):
    if final:
        out_ref, ybuf_ref, sem = rest
        i = pl.program_id(0) * pl.num_programs(1) + pl.program_id(1)
        last = pl.num_programs(0) * pl.num_programs(1) - 1
    else:
        h_ref, hb_ref, ybuf_ref, sem = rest
        i = pl.program_id(0)
        last = pl.num_programs(0) - 1
    slot = i % 2
    tn = h1_ref.shape[0]

    def start_gather(idx_ref, s):
        _start_row_copies(idx_ref, tn, lambda g, k, hi, lo: pltpu.make_async_copy(
            ys_hbm.at[hi, pl.ds(lo, 1), :], ybuf_ref.at[s, g, pl.ds(k, 1), :], sem.at[s]))

    @pl.when(i == 0)
    def _():
        start_gather(pos_ref, 0)

    pltpu.make_async_copy(ys_hbm.at[pl.ds(0, tn // SUBLANES)], ybuf_ref.at[slot], sem.at[slot]).wait()

    @pl.when(i < last)
    def _():
        start_gather(pos_next_ref, 1 - slot)

    lo, hi = _unpack_bf16_pair(ybuf_ref[slot].reshape(tn, h1_ref.shape[1] // 2), jnp.float32)
    y = _layer_norm_rows(alpha * h1_ref[...] + jnp.concatenate([lo, hi], axis=1), g_ref[0], b_ref[0])
    if final:
        out_ref[0] = y
    else:
        h_ref[...] = y
        hb_ref[...] = y.astype(jnp.bfloat16)


def _combine_final(pos, ys, h1, ln_g, ln_b, layer, alpha, batch, seq, tp):
    n, d = h1.shape
    tn = tp - seq
    per_batch = tp // tn
    nx = seq // tn
    pos3 = pos.reshape(batch * per_batch, 1, tn)
    wmap = lambda b, j: (layer, 0, 0)
    cur = lambda b, j: b * per_batch + j

    def nxt(b, j):
        wrap = j + 1 >= nx
        return jnp.where(wrap, jnp.minimum(b + 1, batch - 1) * per_batch, b * per_batch + j + 1)

    return pl.pallas_call(
        functools.partial(_combine_kernel, alpha=alpha, final=True),
        grid=(batch, nx),
        in_specs=[
            pl.BlockSpec((1, 1, tn), lambda b, j: (cur(b, j), 0, 0), memory_space=pltpu.SMEM),
            pl.BlockSpec((1, 1, tn), lambda b, j: (nxt(b, j), 0, 0), memory_space=pltpu.SMEM),
            pl.BlockSpec(memory_space=pl.ANY),
            pl.BlockSpec((tn, d), lambda b, j: (cur(b, j), 0)),
            _const_spec((1, 1, d), wmap),
            _const_spec((1, 1, d), wmap),
        ],
        out_specs=pl.BlockSpec((1, tn, d), lambda b, j: (b, j, 0)),
        out_shape=jax.ShapeDtypeStruct((batch, seq, d), jnp.float32),
        scratch_shapes=[pltpu.VMEM((2, tn // SUBLANES, SUBLANES, d // 2), jnp.int32),
                        pltpu.SemaphoreType.DMA((2,))],
        compiler_params=_cparams(("arbitrary", "arbitrary")),
        name="combine_final",
    )(pos3, pos3, ys, h1, ln_g, ln_b)


def _combine(pos, ys, h1, ln_g, ln_b, layer, alpha, tn):
    n, d = h1.shape
    steps = n // tn
    pos3 = pos.reshape(steps, 1, tn)
    wmap = lambda i: (layer, 0, 0)
    rmap = lambda i: (i, 0)
    return pl.pallas_call(
        functools.partial(_combine_kernel, alpha=alpha, final=False),
        grid=(steps,),
        in_specs=[
            pl.BlockSpec((1, 1, tn), lambda i: (i, 0, 0), memory_space=pltpu.SMEM),
            pl.BlockSpec((1, 1, tn), lambda i: (jnp.minimum(i + 1, steps - 1), 0, 0), memory_space=pltpu.SMEM),
            pl.BlockSpec(memory_space=pl.ANY),
            pl.BlockSpec((tn, d), rmap),
            _const_spec((1, 1, d), wmap),
            _const_spec((1, 1, d), wmap),
        ],
        out_specs=[pl.BlockSpec((tn, d), rmap), pl.BlockSpec((tn, d), rmap)],
        out_shape=[jax.ShapeDtypeStruct((n, d), jnp.float32), jax.ShapeDtypeStruct((n, d), jnp.bfloat16)],
        scratch_shapes=[pltpu.VMEM((2, tn // SUBLANES, SUBLANES, d // 2), jnp.int32),
                        pltpu.SemaphoreType.DMA((2,))],
        compiler_params=_cparams(("arbitrary",)),
        name="combine",
    )(pos3, pos3, ys, h1, ln_g, ln_b)


def _rope_tables(seq, tp):
    rows = seq // GRID_W
    pad = tp - seq - N_META
    pos_row = jnp.concatenate([jnp.repeat(jnp.arange(rows, dtype=jnp.float32), GRID_W),
                               jnp.full((N_META,), -1.0, jnp.float32), jnp.zeros((pad,), jnp.float32)])
    pos_col = jnp.concatenate([jnp.tile(jnp.arange(GRID_W, dtype=jnp.float32), rows),
                               jnp.arange(N_META, dtype=jnp.float32), jnp.zeros((pad,), jnp.float32)])

    def tables(rot_dim):
        n = rot_dim // 4
        inv = ROPE_THETA ** (-jnp.arange(n, dtype=jnp.float32) / n)
        ang = jnp.concatenate([pos_row[:, None] * inv, pos_col[:, None] * inv], axis=-1)
        cos, sin = jnp.cos(ang), jnp.sin(ang)
        reps = LANES // rot_dim
        return (jnp.tile(jnp.concatenate([cos, cos], axis=-1), (1, reps)),
                jnp.tile(jnp.concatenate([-sin, sin], axis=-1), (1, reps)))

    cos_a, sin_a = tables(ROPE_DIM)
    cos_b, sin_b = tables(GQA_HEAD_DIM)
    return cos_a, sin_a, cos_b, sin_b


def _relayout_weights(w_in, w_q_b, w_kv_b):
    bf16 = jnp.bfloat16
    off_kr = Q_LORA + KV_LORA
    off_gq = off_kr + ROPE_DIM
    kr = w_in[:, :, off_kr:off_gq]
    w_in_p = jnp.concatenate([w_in[:, :, :off_kr], kr, kr, w_in[:, :, off_gq:]], axis=-1).astype(bf16)
    depth = w_in.shape[0]
    wq = w_q_b.reshape(depth, Q_LORA, MLA_HEADS, QK_HEAD)
    wq_p = jnp.concatenate([wq[..., :NOPE_DIM].reshape(depth, Q_LORA, -1),
                            wq[..., NOPE_DIM:].reshape(depth, Q_LORA, -1)], axis=-1).astype(bf16)
    wkv = w_kv_b.reshape(depth, KV_LORA, MLA_HEADS, NOPE_DIM + V_DIM)
    wkv_p = jnp.concatenate([wkv[..., :NOPE_DIM].reshape(depth, KV_LORA, -1),
                             wkv[..., NOPE_DIM:].reshape(depth, KV_LORA, -1)], axis=-1).astype(bf16)
    return w_in_p, wq_p, wkv_p


def _dispatch_plan(ri, counts, max_tiles, tp, n_tok):
    tm = EXPERT_TILE
    n = ri.shape[0]
    rows = max_tiles * tm
    cls, rank = ri[:, 0], ri[:, 1]
    cnt = counts[0, :N_CLASSES].astype(jnp.int32)
    tiles_c = (cnt + tm - 1) // tm
    tile_end = jnp.cumsum(tiles_c)
    tile_start = tile_end - tiles_c
    n_tiles = tile_end[-1]
    routed = cls >= 0
    tok = np.arange(n, dtype=np.int32)
    spare = jnp.asarray(rows + (tok // tp) * (tp - n_tok) + (tok % tp - n_tok), dtype=jnp.int32)
    in_class = cls[None, :] == jnp.arange(N_CLASSES, dtype=jnp.int32)[:, None]
    sorted_pos = jnp.sum(jnp.where(in_class, (tile_start * tm)[:, None], 0), axis=0) + rank
    pos_scatter = jnp.where(routed, sorted_pos, spare).astype(jnp.int32)
    pos_gather = jnp.where(routed, sorted_pos, 0).astype(jnp.int32)

    t = jnp.minimum(jnp.arange(max_tiles, dtype=jnp.int32), n_tiles - 1)
    tile_cls = jnp.sum((t[:, None] >= tile_end[None, :]).astype(jnp.int32), axis=1)
    tile_cls = jnp.minimum(tile_cls, N_CLASSES - 1)
    pair_lo = jnp.array([0, 0, 0, 1, 1, 2], jnp.int32)
    pair_hi = jnp.array([1, 2, 3, 2, 3, 3], jnp.int32)
    e_a = EXPERTS_PER_GROUP * (tile_cls // PAIRS_PER_GROUP) + pair_lo[tile_cls % PAIRS_PER_GROUP]
    e_b = EXPERTS_PER_GROUP * (tile_cls // PAIRS_PER_GROUP) + pair_hi[tile_cls % PAIRS_PER_GROUP]
    odd = (jnp.arange(max_tiles) % 2) == 1
    tile_e = jnp.stack([jnp.where(odd, e_b, e_a), jnp.where(odd, e_a, e_b)], axis=-1).reshape(-1)
    step_tile = jnp.arange(2 * max_tiles) // 2
    tile_e = jnp.where(step_tile >= n_tiles, tile_e[2 * n_tiles - 1], tile_e).astype(jnp.int32)
    n_steps = 2 * max_tiles
    step = jnp.arange(n_steps, dtype=jnp.int32)
    used = step_tile < n_tiles
    prev_e = jnp.concatenate([jnp.full((1,), -1, jnp.int32), tile_e[:-1]])
    run_start = used & (tile_e != prev_e)
    slot = jnp.cumsum(run_start.astype(jnp.int32)) % 2
    later_start = run_start[None, :] & (step[None, :] > step[:, None])
    next_start = jnp.min(jnp.where(later_start, step[None, :], n_steps), axis=1)
    next_e = jnp.where(next_start < n_steps, tile_e[jnp.minimum(next_start, n_steps - 1)], -1)
    plan = jnp.concatenate([tile_e, run_start.astype(jnp.int32), slot, next_e]).astype(jnp.int32)
    meta = jnp.stack([n_tiles, n_tiles]).astype(jnp.int32)
    return plan, meta, pos_scatter, pos_gather


def kernel(x, meta_tokens, ln_in_g, ln_in_b, w_in, g_q_lora, w_q_b, g_kv_lora, w_kv_b, g_qk_q, g_qk_k,
           g_out_mla, g_out_gqa, w_out, ln1_g, ln1_b, w_router, router_bias, w_gate, w_up, w_down,
           ln2_g, ln2_b):
    batch, seq, d = x.shape
    depth = w_in.shape[0]
    n_tok = seq + N_META
    tp = -(-n_tok // LANES) * LANES
    n = batch * tp
    alpha = (2.0 * depth) ** 0.25
    bf16 = jnp.bfloat16
    f32 = jnp.float32

    tabs = _rope_tables(seq, tp)
    w_in_p, wq_p, wkv_p = _relayout_weights(w_in, w_q_b, w_kv_b)
    w_out_b = w_out.astype(bf16)
    n_exp, ff = w_gate.shape[1], w_gate.shape[3]
    wg_rows = w_gate.reshape(depth * n_exp * d, ff)
    wu_rows = w_up.reshape(depth * n_exp * d, ff)
    wd_rows = w_down.reshape(depth * n_exp * ff, d)
    wr = jnp.pad(w_router.astype(f32), ((0, 0), (0, LANES - N_EXPERTS)))
    wr_hi = wr.astype(bf16)
    wr_lo = (wr - wr_hi.astype(f32)).astype(bf16)
    wr_hl = jnp.concatenate([wr_hi, wr_lo], axis=1)
    rbias = jnp.pad(router_bias.astype(f32), (0, LANES - N_EXPERTS)).reshape(1, LANES)
    row3 = lambda a: a.reshape(depth, 1, a.shape[-1])

    h, hb = _ln_in(x, meta_tokens.astype(x.dtype), ln_in_g.reshape(1, d), ln_in_b.reshape(1, d), tp)

    tm_e = EXPERT_TILE
    max_tiles = -(-(batch * n_tok) // tm_e) + N_CLASSES
    spare_rows = -(-(batch * (tp - n_tok)) // tm_e) * tm_e
    tn = _row_tile(n, 256)
    xs_rows, row_w = max_tiles * tm_e + spare_rows, d // 2 + LANES
    xs = jnp.zeros((xs_rows // SUBLANES, SUBLANES, row_w), jnp.int32)
    for l in range(depth):
        q_a, k_a, v_a, q_b, k_b, v_b = _in_proj(
            hb, w_in_p, wq_p, wkv_p, row3(g_q_lora), row3(g_kv_lora), row3(g_qk_q), row3(g_qk_k),
            tabs, l, batch, tp)
        o_a, (wg_b, wu_b) = _attention(q_a, k_a, v_a, n_tok, [(wg_rows, (l, depth)), (wu_rows, (l, depth))])
        o_b, (wd_b,) = _attention(q_b, k_b, v_b, n_tok, [(wd_rows, (l, depth))])
        wg_b, wu_b = wg_b.reshape(n_exp, d, ff), wu_b.reshape(n_exp, d, ff)
        wd_b = wd_b.reshape(n_exp, ff, d)
        h1, h1p, ri, counts = _out_proj(
            o_a, o_b, h, w_out_b, row3(g_out_mla), row3(g_out_gqa), row3(ln1_g), row3(ln1_b),
            wr_hl, rbias, l, alpha, tp, n_tok)
        tile_e, tmeta, pos_scatter, pos_gather = _dispatch_plan(ri, counts, max_tiles, tp, n_tok)
        xs = _dispatch(pos_scatter.reshape(n // tn, 1, tn), h1p, xs)
        ys = _experts(tile_e, tmeta, xs.reshape(xs_rows, row_w), wg_b, wu_b, wd_b, max_tiles)
        ys = ys.reshape(ys.shape[0] // SUBLANES, SUBLANES, d // 2)
        if l + 1 < depth:
            h, hb = _combine(pos_gather, ys, h1, row3(ln2_g), row3(ln2_b), l, alpha, tn)
        else:
            out = _combine_final(pos_gather, ys, h1, row3(ln2_g), row3(ln2_b), l, alpha, batch, seq, tp)
    return out
```

```python
import functools
import math

import jax
import jax.numpy as jnp
import numpy as np
from jax import lax
from jax.experimental import pallas as pl
from jax.experimental.pallas import tpu as pltpu

N_META = 16
GRID_W = 64
ROPE_THETA = 10000.0
EPS = 1e-6

MLA_HEADS = 8
Q_LORA = 512
KV_LORA = 256
NOPE_DIM = 128
ROPE_DIM = 64
V_DIM = 128
QK_HEAD = NOPE_DIM + ROPE_DIM
MLA_WIDTH = MLA_HEADS * V_DIM
LOG2_E = math.log2(math.e)
MLA_SCALE = LOG2_E / math.sqrt(QK_HEAD)
MLA_QK_PAD = 256

GQA_HEADS = 8
GQA_KV_HEADS = 2
GQA_HEAD_DIM = 128
GQA_WIDTH = GQA_HEADS * GQA_HEAD_DIM
GQA_SCALE = LOG2_E / math.sqrt(GQA_HEAD_DIM)

N_EXPERTS = 16
N_GROUPS = 4
EXPERTS_PER_GROUP = 4
PAIRS_PER_GROUP = 6
N_CLASSES = N_GROUPS * PAIRS_PER_GROUP
EXPERT_FF = 1024

LANES = 128
VMEM_LIMIT_BYTES = 58 * 1024 * 1024
EXPERT_TILE = 256
NEG_BIG = -1e30
HI16 = -65536

C_CQ = 0
C_CKV = C_CQ + Q_LORA
C_KR = C_CKV + KV_LORA
C_GQ = C_KR + LANES
C_GK = C_GQ + GQA_WIDTH
C_GV = C_GK + GQA_KV_HEADS * GQA_HEAD_DIM
IN_COLS_PAD = C_GV + GQA_KV_HEADS * GQA_HEAD_DIM


def _cparams(semantics, flags=None):
    return pltpu.CompilerParams(dimension_semantics=semantics, vmem_limit_bytes=VMEM_LIMIT_BYTES, flags=flags)


def _const_spec(block_shape, index_map):
    return pl.BlockSpec(block_shape, index_map, pipeline_mode=pl.Buffered(1))


def _row_tile(n, cap, mult=16):
    best = mult
    for t in range(mult, cap + 1, mult):
        if n % t == 0:
            best = t
    assert n % best == 0
    return best


def _layer_norm_rows(z, g, b):
    mu = jnp.mean(z, axis=-1, keepdims=True)
    zc = z - mu
    var = jnp.mean(zc * zc, axis=-1, keepdims=True)
    return zc * lax.rsqrt(var + EPS) * g + b


def _rms_rows(z, g):
    return z * lax.rsqrt(jnp.mean(z * z, axis=-1, keepdims=True) + EPS) * g


def _dot(a, b):
    return jnp.dot(a, b, preferred_element_type=jnp.float32)


def _pack_bf16_pair(lo_f32, hi_f32):
    lo_bits = lax.bitcast_convert_type(lo_f32.astype(jnp.bfloat16).astype(jnp.float32), jnp.int32)
    hi_bits = lax.bitcast_convert_type(hi_f32.astype(jnp.bfloat16).astype(jnp.float32), jnp.int32)
    return lax.shift_right_logical(lo_bits, 16) | (hi_bits & HI16)


def _unpack_bf16_pair(words, dtype=jnp.bfloat16):
    lo = lax.bitcast_convert_type(words << 16, jnp.float32).astype(dtype)
    hi = lax.bitcast_convert_type(words & HI16, jnp.float32).astype(dtype)
    return lo, hi


def _ln_in_kernel(x_ref, meta_ref, g_ref, b_ref, h_ref, hb_ref):
    j = pl.program_id(1)
    last = pl.num_programs(1) - 1

    def emit(rows):
        y = _layer_norm_rows(rows, g_ref[...], b_ref[...])
        h_ref[...] = y
        hb_ref[...] = y.astype(jnp.bfloat16)

    @pl.when(j < last)
    def _():
        emit(x_ref[0])

    @pl.when(j == last)
    def _():
        meta = meta_ref[...]
        pad = jnp.zeros((h_ref.shape[0] - meta.shape[0], meta.shape[1]), meta.dtype)
        emit(jnp.concatenate([meta, pad], axis=0))


def _ln_in(x, meta_tokens, g, b, tp):
    batch, seq, d = x.shape
    tm = tp - seq
    assert seq % tm == 0 and meta_tokens.shape[0] <= tm
    nt = tp // tm
    n = batch * tp
    omap = lambda bi, j: (bi * nt + j, 0)
    cmap = lambda bi, j: (0, 0)
    return pl.pallas_call(
        _ln_in_kernel,
        grid=(batch, nt),
        in_specs=[
            pl.BlockSpec((1, tm, d), lambda bi, j: (bi, jnp.minimum(j, nt - 2), 0)),
            pl.BlockSpec(meta_tokens.shape, cmap),
            pl.BlockSpec((1, d), cmap),
            pl.BlockSpec((1, d), cmap),
        ],
        out_specs=[pl.BlockSpec((tm, d), omap), pl.BlockSpec((tm, d), omap)],
        out_shape=[jax.ShapeDtypeStruct((n, d), jnp.float32), jax.ShapeDtypeStruct((n, d), jnp.bfloat16)],
        compiler_params=_cparams(("parallel", "parallel")),
        name="ln_in",
    )(x, meta_tokens, g, b)


def _swap_halves_64(x):
    lane = lax.broadcasted_iota(jnp.int32, x.shape, 1)
    fwd = pltpu.roll(x, 32, 1)
    bwd = pltpu.roll(x, 96, 1)
    return jnp.where((lane & 63) < 32, bwd, fwd)


def _in_proj_kernel(x_ref, w_in_ref, wq_ref, wkv_ref, gq_ref, gkv_ref, gqq_ref, gqk_ref,
                    cos_a_ref, sin_a_ref, cos_b_ref, sin_b_ref,
                    q_mla_ref, k_mla_ref, v_mla_ref, q_gqa_ref, k_gqa_ref, v_gqa_ref):
    bf16 = jnp.bfloat16
    u = _dot(x_ref[...], w_in_ref[0])
    cos_a, sin_a = cos_a_ref[...], sin_a_ref[...]
    cos_b, sin_b = cos_b_ref[...], sin_b_ref[...]
    lane = lax.broadcasted_iota(jnp.int32, cos_a.shape, 1)

    c_q = _rms_rows(u[:, C_CQ:C_CKV], gq_ref[0]).astype(bf16)
    qa = _dot(c_q, wq_ref[0])
    c_kv = _rms_rows(u[:, C_CKV:C_KR], gkv_ref[0]).astype(bf16)
    kv = _dot(c_kv, wkv_ref[0])
    kr = u[:, C_KR:C_GQ]
    kr = kr * cos_a + _swap_halves_64(kr) * sin_a
    rope_base = MLA_HEADS * NOPE_DIM
    for pair in range(MLA_HEADS // 2):
        blk = qa[:, rope_base + LANES * pair:rope_base + LANES * (pair + 1)]
        rot = blk * cos_a + _swap_halves_64(blk) * sin_a
        for half in range(2):
            h = 2 * pair + half
            own = jnp.where((lane >> 6) == half, rot, 0.0)
            qh = jnp.concatenate([qa[:, NOPE_DIM * h:NOPE_DIM * (h + 1)], own], axis=1) * MLA_SCALE
            q_mla_ref[0, h] = qh.astype(bf16)
    for h in range(MLA_HEADS):
        kh = jnp.concatenate([kv[:, NOPE_DIM * h:NOPE_DIM * (h + 1)], kr], axis=1)
        k_mla_ref[0, h] = kh.astype(bf16)
        v0 = MLA_HEADS * NOPE_DIM + V_DIM * h
        v_mla_ref[0, h] = kv[:, v0:v0 + V_DIM].astype(bf16)

    for h in range(GQA_HEADS):
        qh = _rms_rows(u[:, C_GQ + GQA_HEAD_DIM * h:C_GQ + GQA_HEAD_DIM * (h + 1)], gqq_ref[0])
        qh = qh * cos_b + pltpu.roll(qh, 64, 1) * sin_b
        q_gqa_ref[0, h] = (qh * GQA_SCALE).astype(bf16)
    for h in range(GQA_KV_HEADS):
        kh = _rms_rows(u[:, C_GK + GQA_HEAD_DIM * h:C_GK + GQA_HEAD_DIM * (h + 1)], gqk_ref[0])
        kh = kh * cos_b + pltpu.roll(kh, 64, 1) * sin_b
        k_gqa_ref[0, h] = kh.astype(bf16)
        v_gqa_ref[0, h] = u[:, C_GV + GQA_HEAD_DIM * h:C_GV + GQA_HEAD_DIM * (h + 1)].astype(bf16)


def _in_proj(hb, w_in_p, wq_p, wkv_p, g_q, g_kv, g_qq, g_qk, tabs, layer, batch, tp):
    n, d = hb.shape
    tm = _row_tile(tp, 320)
    nt = tp // tm
    cos_a, sin_a, cos_b, sin_b = tabs
    wmap = lambda b, i: (layer, 0, 0)
    tmap = lambda b, i: (i, 0)
    omap = lambda b, i: (b, 0, i, 0)
    bf16 = jnp.bfloat16
    return pl.pallas_call(
        _in_proj_kernel,
        grid=(batch, nt),
        in_specs=[
            pl.BlockSpec((tm, d), lambda b, i: (b * nt + i, 0)),
            _const_spec((1, d, IN_COLS_PAD), wmap),
            _const_spec((1, Q_LORA, wq_p.shape[2]), wmap),
            _const_spec((1, KV_LORA, wkv_p.shape[2]), wmap),
            _const_spec((1, 1, Q_LORA), wmap),
            _const_spec((1, 1, KV_LORA), wmap),
            _const_spec((1, 1, GQA_HEAD_DIM), wmap),
            _const_spec((1, 1, GQA_HEAD_DIM), wmap),
            pl.BlockSpec((tm, LANES), tmap),
            pl.BlockSpec((tm, LANES), tmap),
            pl.BlockSpec((tm, LANES), tmap),
            pl.BlockSpec((tm, LANES), tmap),
        ],
        out_specs=[
            pl.BlockSpec((1, MLA_HEADS, tm, MLA_QK_PAD), omap),
            pl.BlockSpec((1, MLA_HEADS, tm, MLA_QK_PAD), omap),
            pl.BlockSpec((1, MLA_HEADS, tm, V_DIM), omap),
            pl.BlockSpec((1, GQA_HEADS, tm, GQA_HEAD_DIM), omap),
            pl.BlockSpec((1, GQA_KV_HEADS, tm, GQA_HEAD_DIM), omap),
            pl.BlockSpec((1, GQA_KV_HEADS, tm, GQA_HEAD_DIM), omap),
        ],
        out_shape=[
            jax.ShapeDtypeStruct((batch, MLA_HEADS, tp, MLA_QK_PAD), bf16),
            jax.ShapeDtypeStruct((batch, MLA_HEADS, tp, MLA_QK_PAD), bf16),
            jax.ShapeDtypeStruct((batch, MLA_HEADS, tp, V_DIM), bf16),
            jax.ShapeDtypeStruct((batch, GQA_HEADS, tp, GQA_HEAD_DIM), bf16),
            jax.ShapeDtypeStruct((batch, GQA_KV_HEADS, tp, GQA_HEAD_DIM), bf16),
            jax.ShapeDtypeStruct((batch, GQA_KV_HEADS, tp, GQA_HEAD_DIM), bf16),
        ],
        compiler_params=_cparams(("parallel", "parallel")),
        name="in_proj",
    )(hb, w_in_p, wq_p, wkv_p, g_q, g_kv, g_qq, g_qk, cos_a, sin_a, cos_b, sin_b)


def _attn_kernel(q_ref, k_ref, v_ref, *rest, tq, n_keys, n_cast):
    o_ref = rest[n_cast]
    for src_ref, dst_ref in zip(rest[:n_cast], rest[n_cast + 1:]):
        dst_ref[...] = src_ref[...].astype(dst_ref.dtype)
    tp = k_ref.shape[2]
    body = tp - LANES
    nt = (((1,), (1,)), ((), ()))
    k_body, k_tail = k_ref[0, 0, :body, :], k_ref[0, 0, body:, :]
    v_body, v_tail = v_ref[0, 0, :body, :], v_ref[0, 0, body:, :]
    tail_ok = lax.broadcasted_iota(jnp.int32, (1, LANES), 1) < (n_keys - body)
    n_q = -(-n_keys // 16) * 16
    if n_q < tp:
        o_ref[n_q:, :] = jnp.zeros((tp - n_q, o_ref.shape[1]), o_ref.dtype)
    for r0 in range(0, n_q, tq):
        rows = slice(r0, min(r0 + tq, n_q))
        q = q_ref[0, 0, rows, :]
        s_body = lax.dot_general(q, k_body, nt, preferred_element_type=jnp.float32)
        s_tail = lax.dot_general(q, k_tail, nt, preferred_element_type=jnp.float32)
        s_tail = jnp.where(tail_ok, s_tail, NEG_BIG)
        m = jnp.maximum(jnp.max(s_body, axis=-1, keepdims=True), jnp.max(s_tail, axis=-1, keepdims=True))
        p_body = jnp.exp2(s_body - m)
        p_tail = jnp.exp2(s_tail - m)
        l = jnp.sum(p_body, axis=-1, keepdims=True) + jnp.sum(p_tail, axis=-1, keepdims=True)
        o = _dot(p_body.astype(jnp.bfloat16), v_body) + _dot(p_tail.astype(jnp.bfloat16), v_tail)
        o_ref[rows, :] = (o / l).astype(o_ref.dtype)


def _attention(q, k, v, n_keys, casts):
    batch, hq, tp, dk = q.shape
    hk, dv = k.shape[1], v.shape[3]
    rep = hq // hk
    tq = _row_tile(tp, 320)
    steps = batch * hq
    cast_in, cast_out, cast_shapes = [], [], []
    for w, layer in casts:
        rows = w.shape[0] // (layer[1])
        assert rows % steps == 0
        blk = rows // steps
        off = layer[0] * steps
        cast_in.append(pl.BlockSpec((blk, w.shape[1]), lambda b, h, off=off: (off + b * hq + h, 0)))
        cast_out.append(pl.BlockSpec((blk, w.shape[1]), lambda b, h: (b * hq + h, 0)))
        cast_shapes.append(jax.ShapeDtypeStruct((rows, w.shape[1]), jnp.bfloat16))
    outs = pl.pallas_call(
        functools.partial(_attn_kernel, tq=tq, n_keys=n_keys, n_cast=len(casts)),
        grid=(batch, hq),
        in_specs=[
            pl.BlockSpec((1, 1, tp, dk), lambda b, h: (b, h, 0, 0)),
            pl.BlockSpec((1, 1, tp, dk), lambda b, h: (b, h // rep, 0, 0)),
            pl.BlockSpec((1, 1, tp, dv), lambda b, h: (b, h // rep, 0, 0)),
        ] + cast_in,
        out_specs=[pl.BlockSpec((tp, dv), lambda b, h: (b, h))] + cast_out,
        out_shape=[jax.ShapeDtypeStruct((batch * tp, hq * dv), jnp.bfloat16)] + cast_shapes,
        compiler_params=_cparams(("parallel", "parallel")),
        name="attention",
    )(q, k, v, *[w for w, _ in casts])
    return outs[0], outs[1:]


def _first_index_of_max(vals, lane_f):
    m = jnp.max(vals, axis=-1, keepdims=True)
    idx = jnp.min(jnp.where(vals == m, lane_f, float(LANES)), axis=-1, keepdims=True)
    return m, idx


def _out_proj_kernel(oa_ref, ob_ref, h_ref, w_out_ref, ga_ref, gb_ref, lg_ref, lb_ref,
                     wr_hl_ref, rbias_ref,
                     h1_ref, h1p_ref, ri_ref, cnt_ref, carry_ref, *, alpha, tp, n_tok, sub):
    i = pl.program_id(0)
    tm = h_ref.shape[0]

    @pl.when(i == 0)
    def _():
        carry_ref[...] = jnp.zeros_like(carry_ref)

    blocks = [slice(r0, r0 + sub) for r0 in range(0, tm, sub)]
    mixed = [_out_proj_matmul(rows, oa_ref, ob_ref, w_out_ref, ga_ref, gb_ref) for rows in blocks]
    routed_in = [_out_proj_norm(rows, m, h_ref, lg_ref, lb_ref, wr_hl_ref, h1_ref, alpha=alpha)
                 for rows, m in zip(blocks, mixed)]
    for rows, (h1, logits) in zip(blocks, routed_in):
        _out_proj_route(rows, i * tm + rows.start, h1, logits, rbias_ref, h1p_ref, ri_ref, carry_ref,
                        tp=tp, n_tok=n_tok)
    cnt_ref[...] = carry_ref[...]


def _out_proj_matmul(rows, oa_ref, ob_ref, w_out_ref, ga_ref, gb_ref):
    bf16 = jnp.bfloat16
    na = _rms_rows(oa_ref[rows, :].astype(jnp.float32), ga_ref[0]).astype(bf16)
    nb = _rms_rows(ob_ref[rows, :].astype(jnp.float32), gb_ref[0]).astype(bf16)
    return _dot(na, w_out_ref[0, :MLA_WIDTH, :]) + _dot(nb, w_out_ref[0, MLA_WIDTH:, :])


def _out_proj_norm(rows, mixed, h_ref, lg_ref, lb_ref, wr_hl_ref, h1_ref, *, alpha):
    bf16 = jnp.bfloat16
    h1 = _layer_norm_rows(alpha * h_ref[rows, :] + mixed, lg_ref[0], lb_ref[0])
    h1_ref[rows, :] = h1
    hi = h1.astype(bf16)
    lo = (h1 - hi.astype(jnp.float32)).astype(bf16)
    hi_prod = _dot(hi, wr_hl_ref[...])
    logits = hi_prod[:, :LANES] + hi_prod[:, LANES:] + _dot(lo, wr_hl_ref[:, :LANES])
    return h1, logits


def _out_proj_route(rows, row0, h1, logits, rbias_ref, h1p_ref, ri_ref, carry_ref, *, tp, n_tok):
    bf16 = jnp.bfloat16
    f32 = jnp.float32
    tm = rows.stop - rows.start
    half = h1.shape[1] // 2
    scores = jax.nn.sigmoid(logits)

    lane = lax.broadcasted_iota(jnp.int32, (tm, LANES), 1)
    lane_f = lane.astype(f32)
    neg = -jnp.inf
    sel = jnp.where(lane < N_EXPERTS, scores + rbias_ref[...], neg)
    grp = lane >> 2

    best = None
    for g in range(N_GROUPS):
        mg = jnp.where(grp == g, sel, neg)
        m1, i1 = _first_index_of_max(mg, lane_f)
        m2 = jnp.max(jnp.where(lane_f == i1, neg, mg), axis=-1, keepdims=True)
        gs = m1 + m2
        if best is None:
            best, gi = gs, jnp.zeros_like(gs)
        else:
            better = gs > best
            gi = jnp.where(better, float(g), gi)
            best = jnp.where(better, gs, best)

    mg = jnp.where(grp.astype(f32) == gi, sel, neg)
    _, e1 = _first_index_of_max(mg, lane_f)
    mg2 = jnp.where(lane_f == e1, neg, mg)
    _, e2 = _first_index_of_max(mg2, lane_f)
    w1 = jnp.sum(jnp.where(lane_f == e1, scores, 0.0), axis=-1, keepdims=True)
    w2 = jnp.sum(jnp.where(lane_f == e2, scores, 0.0), axis=-1, keepdims=True)
    den = w1 + w2
    w1, w2 = w1 / den, w2 / den

    first_lower = e1 < e2
    la = jnp.where(first_lower, e1, e2) - EXPERTS_PER_GROUP * gi
    lb = jnp.where(first_lower, e2, e1) - EXPERTS_PER_GROUP * gi
    w_a = jnp.where(first_lower, w1, w2)
    w_b = jnp.where(first_lower, w2, w1)
    cls = PAIRS_PER_GROUP * gi + la * (7.0 - la) * 0.5 + (lb - la - 1.0)

    row = (row0 + lax.broadcasted_iota(jnp.int32, (tm, 1), 0)).astype(f32)
    routed = (row - jnp.floor((row + 0.5) * (1.0 / tp)) * tp) < n_tok
    onehot = jnp.where((lane_f == cls) & routed, 1.0, 0.0)

    r_i = lax.broadcasted_iota(jnp.int32, (tm, tm), 0)
    c_i = lax.broadcasted_iota(jnp.int32, (tm, tm), 1)
    lower = jnp.where(c_i < r_i, 1.0, 0.0).astype(bf16)
    before = _dot(lower, onehot.astype(bf16)) + carry_ref[...]
    rank = jnp.sum(onehot * before, axis=-1, keepdims=True)
    carry_ref[...] += jnp.sum(onehot, axis=0, keepdims=True)

    cls_out = jnp.where(routed, cls, -1.0)
    ri_ref[rows, :] = jnp.where(lane == 0, cls_out, jnp.where(lane == 1, rank, 0.0)).astype(jnp.int32)

    h1p_ref[rows, :half] = _pack_bf16_pair(h1[:, :half], h1[:, half:])
    weights = jnp.where(lane == 0, w_a, jnp.where(lane == 1, w_b, 0.0))
    h1p_ref[rows, half:] = lax.bitcast_convert_type(weights, jnp.int32)


def _out_proj(o_a, o_b, h, w_out_b, g_a, g_b, ln_g, ln_b, wr_hl, rbias, layer, alpha, tp, n_tok):
    n, d = h.shape
    sub = _row_tile(n, 256)
    tm = 2 * sub if n % (2 * sub) == 0 else sub
    row_w = d // 2 + LANES
    wmap = lambda i: (layer, 0, 0)
    cmap = lambda i: (0, 0)
    rmap = lambda i: (i, 0)
    return pl.pallas_call(
        functools.partial(_out_proj_kernel, alpha=alpha, tp=tp, n_tok=n_tok, sub=sub),
        grid=(n // tm,),
        in_specs=[
            pl.BlockSpec((tm, MLA_WIDTH), rmap),
            pl.BlockSpec((tm, GQA_WIDTH), rmap),
            pl.BlockSpec((tm, d), rmap),
            _const_spec((1, MLA_WIDTH + GQA_WIDTH, d), wmap),
            _const_spec((1, 1, MLA_WIDTH), wmap),
            _const_spec((1, 1, GQA_WIDTH), wmap),
            _const_spec((1, 1, d), wmap),
            _const_spec((1, 1, d), wmap),
            _const_spec((d, 2 * LANES), cmap),
            _const_spec((1, LANES), cmap),
        ],
        out_specs=[
            pl.BlockSpec((tm, d), rmap),
            pl.BlockSpec((tm, row_w), rmap),
            pl.BlockSpec((tm, LANES), rmap),
            pl.BlockSpec((1, LANES), cmap),
        ],
        out_shape=[
            jax.ShapeDtypeStruct((n, d), jnp.float32),
            jax.ShapeDtypeStruct((n, row_w), jnp.int32),
            jax.ShapeDtypeStruct((n, LANES), jnp.int32),
            jax.ShapeDtypeStruct((1, LANES), jnp.float32),
        ],
        scratch_shapes=[pltpu.VMEM((1, LANES), jnp.float32)],
        compiler_params=_cparams(("arbitrary",)),
        name="out_proj",
    )(o_a, o_b, h, w_out_b, g_a, g_b, ln_g, ln_b, wr_hl, rbias)


SUBLANES = 8


def _start_row_copies(idx_ref, n_rows, make_copy):
    def issue(g, carry):
        for k in range(SUBLANES):
            idx = idx_ref[0, 0, g * SUBLANES + k]
            make_copy(g, k, lax.shift_right_logical(idx, 3), idx & (SUBLANES - 1)).start()
        return carry

    lax.fori_loop(0, n_rows // SUBLANES, issue, 0)


def _dispatch_kernel(pos_ref, x_ref, xs_init_hbm, xs_hbm, buf_ref, sem):
    del xs_init_hbm
    i = pl.program_id(0)
    last = pl.num_programs(0) - 1
    slot = i % 2
    tn = x_ref.shape[0]
    groups = tn // SUBLANES

    def wait_slot(s):
        pltpu.make_async_copy(buf_ref.at[s], xs_hbm.at[pl.ds(0, groups)], sem.at[s]).wait()

    @pl.when(i >= 2)
    def _():
        wait_slot(slot)

    buf_ref[slot] = x_ref[...].reshape(groups, SUBLANES, x_ref.shape[1])
    _start_row_copies(pos_ref, tn, lambda g, k, hi, lo: pltpu.make_async_copy(
        buf_ref.at[slot, g, pl.ds(k, 1), :], xs_hbm.at[hi, pl.ds(lo, 1), :], sem.at[slot]))

    @pl.when(i == last)
    def _():
        wait_slot(slot)

        @pl.when(last >= 1)
        def _():
            wait_slot(1 - slot)


def _dispatch(pos3, h1p, xs_init):
    n, row_w = h1p.shape
    tn = pos3.shape[2]
    return pl.pallas_call(
        _dispatch_kernel,
        grid=(n // tn,),
        in_specs=[
            pl.BlockSpec((1, 1, tn), lambda i: (i, 0, 0), memory_space=pltpu.SMEM),
            pl.BlockSpec((tn, row_w), lambda i: (i, 0)),
            pl.BlockSpec(memory_space=pl.ANY),
        ],
        out_specs=pl.BlockSpec(memory_space=pl.ANY),
        out_shape=jax.ShapeDtypeStruct(xs_init.shape, xs_init.dtype),
        scratch_shapes=[pltpu.VMEM((2, tn // SUBLANES, SUBLANES, row_w), h1p.dtype),
                        pltpu.SemaphoreType.DMA((2,))],
        input_output_aliases={2: 0},
        compiler_params=_cparams(("arbitrary",)),
        name="dispatch",
    )(pos3, h1p, xs_init)


def _experts_kernel(plan_ref, meta_ref, xs_ref, wg_hbm, wu_hbm, wd_hbm, ys_ref, acc_ref, wg_buf, wu_buf, wd_buf,
                    sem):
    i = pl.program_id(0)
    k = pl.program_id(1)
    steps = 2 * pl.num_programs(0)
    t = 2 * i + k
    half = wg_buf.shape[1] // 2

    def weight_copies(e, s):
        return (pltpu.make_async_copy(wg_hbm.at[e], wg_buf.at[s], sem.at[s]),
                pltpu.make_async_copy(wu_hbm.at[e], wu_buf.at[s], sem.at[s]),
                pltpu.make_async_copy(wd_hbm.at[e], wd_buf.at[s], sem.at[s]))

    @pl.when(i < meta_ref[0])
    def _():
        slot = plan_ref[2 * steps + t]

        @pl.when(plan_ref[steps + t] == 1)
        def _():
            @pl.when(t == 0)
            def _():
                for c in weight_copies(plan_ref[0], slot):
                    c.start()

            for c in weight_copies(plan_ref[t], slot):
                c.wait()
            nxt = plan_ref[3 * steps + t]

            @pl.when(nxt >= 0)
            def _():
                for c in weight_copies(nxt, 1 - slot):
                    c.start()

        lo, hi = _unpack_bf16_pair(xs_ref[:, :half])
        g = _dot(lo, wg_buf[slot, :half, :]) + _dot(hi, wg_buf[slot, half:, :])
        u = _dot(lo, wu_buf[slot, :half, :]) + _dot(hi, wu_buf[slot, half:, :])
        a = (g * jax.nn.sigmoid(g) * u).astype(jnp.bfloat16)
        y = _dot(a, wd_buf[slot])
        second = (k + i) % 2 == 1
        wab = lax.bitcast_convert_type(xs_ref[:, half:], jnp.float32)
        y = y * jnp.where(second, wab[:, 1:2], wab[:, 0:1])

        @pl.when(k == 0)
        def _():
            acc_ref[...] = y

        @pl.when(k == 1)
        def _():
            total = acc_ref[...] + y
            ys_ref[...] = _pack_bf16_pair(total[:, :half], total[:, half:])

    @pl.when((i >= meta_ref[0]) & (k == 1))
    def _():
        ys_ref[...] = jnp.zeros_like(ys_ref)


def _experts(plan, meta, xs, wg_b, wu_b, wd_b, max_tiles):
    tm = EXPERT_TILE
    row_w = xs.shape[1]
    d, ff = wg_b.shape[1], wg_b.shape[2]

    def xmap(i, k, pr, mt):
        return (jnp.minimum(i, mt[0] - 1), 0)

    grid_spec = pltpu.PrefetchScalarGridSpec(
        num_scalar_prefetch=2,
        grid=(max_tiles, 2),
        in_specs=[
            pl.BlockSpec((tm, row_w), xmap),
            pl.BlockSpec(memory_space=pl.ANY),
            pl.BlockSpec(memory_space=pl.ANY),
            pl.BlockSpec(memory_space=pl.ANY),
        ],
        out_specs=pl.BlockSpec((tm, d // 2), lambda i, k, pr, mt: (i, 0)),
        scratch_shapes=[
            pltpu.VMEM((tm, d), jnp.float32),
            pltpu.VMEM((2, d, ff), wg_b.dtype),
            pltpu.VMEM((2, d, ff), wu_b.dtype),
            pltpu.VMEM((2, ff, d), wd_b.dtype),
            pltpu.SemaphoreType.DMA((2,)),
        ],
    )
    return pl.pallas_call(
        _experts_kernel,
        grid_spec=grid_spec,
        out_shape=jax.ShapeDtypeStruct((max_tiles * tm, d // 2), jnp.int32),
        compiler_params=_cparams(("arbitrary", "arbitrary")),
        name="experts",
    )(plan, meta, xs, wg_b, wu_b, wd_b)


def _combine_kernel(pos_ref, pos_next_ref, ys_hbm, h1_ref, g_ref, b_ref, *rest, alpha, final):
    if final:
        out_ref, ybuf_ref, sem = rest
        i = pl.program_id(0) * pl.num_programs(1) + pl.program_id(1)
        last = pl.num_programs(0) * pl.num_programs(1) - 1
    else:
        h_ref, hb_ref, ybuf_ref, sem = rest
        i = pl.program_id(0)
        last = pl.num_programs(0) - 1
    slot = i % 2
    tn = h1_ref.shape[0]

    def start_gather(idx_ref, s):
        _start_row_copies(idx_ref, tn, lambda g, k, hi, lo: pltpu.make_async_copy(
            ys_hbm.at[hi, pl.ds(lo, 1), :], ybuf_ref.at[s, g, pl.ds(k, 1), :], sem.at[s]))

    @pl.when(i == 0)
    def _():
        start_gather(pos_ref, 0)

    pltpu.make_async_copy(ys_hbm.at[pl.ds(0, tn // SUBLANES)], ybuf_ref.at[slot], sem.at[slot]).wait()

    @pl.when(i < last)
    def _():
        start_gather(pos_next_ref, 1 - slot)

    lo, hi = _unpack_bf16_pair(ybuf_ref[slot].reshape(tn, h1_ref.shape[1] // 2), jnp.float32)
    y = _layer_norm_rows(alpha * h1_ref[...] + jnp.concatenate([lo, hi], axis=1), g_ref[0], b_ref[0])
    if final:
        out_ref[0] = y
    else:
        h_ref[...] = y
        hb_ref[...] = y.astype(jnp.bfloat16)


def _combine_final(pos, ys, h1, ln_g, ln_b, layer, alpha, batch, seq, tp):
    n, d = h1.shape
    tn = tp - seq
    per_batch = tp // tn
    nx = seq // tn
    pos3 = pos.reshape(batch * per_batch, 1, tn)
    wmap = lambda b, j: (layer, 0, 0)
    cur = lambda b, j: b * per_batch + j

    def nxt(b, j):
        wrap = j + 1 >= nx
        return jnp.where(wrap, jnp.minimum(b + 1, batch - 1) * per_batch, b * per_batch + j + 1)

    return pl.pallas_call(
        functools.partial(_combine_kernel, alpha=alpha, final=True),
        grid=(batch, nx),
        in_specs=[
            pl.BlockSpec((1, 1, tn), lambda b, j: (cur(b, j), 0, 0), memory_space=pltpu.SMEM),
            pl.BlockSpec((1, 1, tn), lambda b, j: (nxt(b, j), 0, 0), memory_space=pltpu.SMEM),
            pl.BlockSpec(memory_space=pl.ANY),
            pl.BlockSpec((tn, d), lambda b, j: (cur(b, j), 0)),
            _const_spec((1, 1, d), wmap),
            _const_spec((1, 1, d), wmap),
        ],
        out_specs=pl.BlockSpec((1, tn, d), lambda b, j: (b, j, 0)),
        out_shape=jax.ShapeDtypeStruct((batch, seq, d), jnp.float32),
        scratch_shapes=[pltpu.VMEM((2, tn // SUBLANES, SUBLANES, d // 2), jnp.int32),
                        pltpu.SemaphoreType.DMA((2,))],
        compiler_params=_cparams(("arbitrary", "arbitrary")),
        name="combine_final",
    )(pos3, pos3, ys, h1, ln_g, ln_b)


def _combine(pos, ys, h1, ln_g, ln_b, layer, alpha, tn):
    n, d = h1.shape
    steps = n // tn
    pos3 = pos.reshape(steps, 1, tn)
    wmap = lambda i: (layer, 0, 0)
    rmap = lambda i: (i, 0)
    return pl.pallas_call(
        functools.partial(_combine_kernel, alpha=alpha, final=False),
        grid=(steps,),
        in_specs=[
            pl.BlockSpec((1, 1, tn), lambda i: (i, 0, 0), memory_space=pltpu.SMEM),
            pl.BlockSpec((1, 1, tn), lambda i: (jnp.minimum(i + 1, steps - 1), 0, 0), memory_space=pltpu.SMEM),
            pl.BlockSpec(memory_space=pl.ANY),
            pl.BlockSpec((tn, d), rmap),
            _const_spec((1, 1, d), wmap),
            _const_spec((1, 1, d), wmap),
        ],
        out_specs=[pl.BlockSpec((tn, d), rmap), pl.BlockSpec((tn, d), rmap)],
        out_shape=[jax.ShapeDtypeStruct((n, d), jnp.float32), jax.ShapeDtypeStruct((n, d), jnp.bfloat16)],
        scratch_shapes=[pltpu.VMEM((2, tn // SUBLANES, SUBLANES, d // 2), jnp.int32),
                        pltpu.SemaphoreType.DMA((2,))],
        compiler_params=_cparams(("arbitrary",)),
        name="combine",
    )(pos3, pos3, ys, h1, ln_g, ln_b)


def _rope_tables(seq, tp):
    rows = seq // GRID_W
    pad = tp - seq - N_META
    pos_row = jnp.concatenate([jnp.repeat(jnp.arange(rows, dtype=jnp.float32), GRID_W),
                               jnp.full((N_META,), -1.0, jnp.float32), jnp.zeros((pad,), jnp.float32)])
    pos_col = jnp.concatenate([jnp.tile(jnp.arange(GRID_W, dtype=jnp.float32), rows),
                               jnp.arange(N_META, dtype=jnp.float32), jnp.zeros((pad,), jnp.float32)])

    def tables(rot_dim):
        n = rot_dim // 4
        inv = ROPE_THETA ** (-jnp.arange(n, dtype=jnp.float32) / n)
        ang = jnp.concatenate([pos_row[:, None] * inv, pos_col[:, None] * inv], axis=-1)
        cos, sin = jnp.cos(ang), jnp.sin(ang)
        reps = LANES // rot_dim
        return (jnp.tile(jnp.concatenate([cos, cos], axis=-1), (1, reps)),
                jnp.tile(jnp.concatenate([-sin, sin], axis=-1), (1, reps)))

    cos_a, sin_a = tables(ROPE_DIM)
    cos_b, sin_b = tables(GQA_HEAD_DIM)
    return cos_a, sin_a, cos_b, sin_b


def _relayout_weights(w_in, w_q_b, w_kv_b):
    bf16 = jnp.bfloat16
    off_kr = Q_LORA + KV_LORA
    off_gq = off_kr + ROPE_DIM
    kr = w_in[:, :, off_kr:off_gq]
    w_in_p = jnp.concatenate([w_in[:, :, :off_kr], kr, kr, w_in[:, :, off_gq:]], axis=-1).astype(bf16)
    depth = w_in.shape[0]
    wq = w_q_b.reshape(depth, Q_LORA, MLA_HEADS, QK_HEAD)
    wq_p = jnp.concatenate([wq[..., :NOPE_DIM].reshape(depth, Q_LORA, -1),
                            wq[..., NOPE_DIM:].reshape(depth, Q_LORA, -1)], axis=-1).astype(bf16)
    wkv = w_kv_b.reshape(depth, KV_LORA, MLA_HEADS, NOPE_DIM + V_DIM)
    wkv_p = jnp.concatenate([wkv[..., :NOPE_DIM].reshape(depth, KV_LORA, -1),
                             wkv[..., NOPE_DIM:].reshape(depth, KV_LORA, -1)], axis=-1).astype(bf16)
    return w_in_p, wq_p, wkv_p


def _dispatch_plan(ri, counts, max_tiles, tp, n_tok):
    tm = EXPERT_TILE
    n = ri.shape[0]
    rows = max_tiles * tm
    cls, rank = ri[:, 0], ri[:, 1]
    cnt = counts[0, :N_CLASSES].astype(jnp.int32)
    tiles_c = (cnt + tm - 1) // tm
    tile_end = jnp.cumsum(tiles_c)
    tile_start = tile_end - tiles_c
    n_tiles = tile_end[-1]
    routed = cls >= 0
    tok = np.arange(n, dtype=np.int32)
    spare = jnp.asarray(rows + (tok // tp) * (tp - n_tok) + (tok % tp - n_tok), dtype=jnp.int32)
    in_class = cls[None, :] == jnp.arange(N_CLASSES, dtype=jnp.int32)[:, None]
    sorted_pos = jnp.sum(jnp.where(in_class, (tile_start * tm)[:, None], 0), axis=0) + rank
    pos_scatter = jnp.where(routed, sorted_pos, spare).astype(jnp.int32)
    pos_gather = jnp.where(routed, sorted_pos, 0).astype(jnp.int32)

    t = jnp.minimum(jnp.arange(max_tiles, dtype=jnp.int32), n_tiles - 1)
    tile_cls = jnp.sum((t[:, None] >= tile_end[None, :]).astype(jnp.int32), axis=1)
    tile_cls = jnp.minimum(tile_cls, N_CLASSES - 1)
    pair_lo = jnp.array([0, 0, 0, 1, 1, 2], jnp.int32)
    pair_hi = jnp.array([1, 2, 3, 2, 3, 3], jnp.int32)
    e_a = EXPERTS_PER_GROUP * (tile_cls // PAIRS_PER_GROUP) + pair_lo[tile_cls % PAIRS_PER_GROUP]
    e_b = EXPERTS_PER_GROUP * (tile_cls // PAIRS_PER_GROUP) + pair_hi[tile_cls % PAIRS_PER_GROUP]
    odd = (jnp.arange(max_tiles) % 2) == 1
    tile_e = jnp.stack([jnp.where(odd, e_b, e_a), jnp.where(odd, e_a, e_b)], axis=-1).reshape(-1)
    step_tile = jnp.arange(2 * max_tiles) // 2
    tile_e = jnp.where(step_tile >= n_tiles, tile_e[2 * n_tiles - 1], tile_e).astype(jnp.int32)
    n_steps = 2 * max_tiles
    step = jnp.arange(n_steps, dtype=jnp.int32)
    used = step_tile < n_tiles
    prev_e = jnp.concatenate([jnp.full((1,), -1, jnp.int32), tile_e[:-1]])
    run_start = used & (tile_e != prev_e)
    slot = jnp.cumsum(run_start.astype(jnp.int32)) % 2
    later_start = run_start[None, :] & (step[None, :] > step[:, None])
    next_start = jnp.min(jnp.where(later_start, step[None, :], n_steps), axis=1)
    next_e = jnp.where(next_start < n_steps, tile_e[jnp.minimum(next_start, n_steps - 1)], -1)
    plan = jnp.concatenate([tile_e, run_start.astype(jnp.int32), slot, next_e]).astype(jnp.int32)
    meta = jnp.stack([n_tiles, n_tiles]).astype(jnp.int32)
    return plan, meta, pos_scatter, pos_gather


def kernel(x, meta_tokens, ln_in_g, ln_in_b, w_in, g_q_lora, w_q_b, g_kv_lora, w_kv_b, g_qk_q, g_qk_k,
           g_out_mla, g_out_gqa, w_out, ln1_g, ln1_b, w_router, router_bias, w_gate, w_up, w_down,
           ln2_g, ln2_b):
    batch, seq, d = x.shape
    depth = w_in.shape[0]
    n_tok = seq + N_META
    tp = -(-n_tok // LANES) * LANES
    n = batch * tp
    alpha = (2.0 * depth) ** 0.25
    bf16 = jnp.bfloat16
    f32 = jnp.float32

    tabs = _rope_tables(seq, tp)
    w_in_p, wq_p, wkv_p = _relayout_weights(w_in, w_q_b, w_kv_b)
    w_out_b = w_out.astype(bf16)
    n_exp, ff = w_gate.shape[1], w_gate.shape[3]
    wg_rows = w_gate.reshape(depth * n_exp * d, ff)
    wu_rows = w_up.reshape(depth * n_exp * d, ff)
    wd_rows = w_down.reshape(depth * n_exp * ff, d)
    wr = jnp.pad(w_router.astype(f32), ((0, 0), (0, LANES - N_EXPERTS)))
    wr_hi = wr.astype(bf16)
    wr_lo = (wr - wr_hi.astype(f32)).astype(bf16)
    wr_hl = jnp.concatenate([wr_hi, wr_lo], axis=1)
    rbias = jnp.pad(router_bias.astype(f32), (0, LANES - N_EXPERTS)).reshape(1, LANES)
    row3 = lambda a: a.reshape(depth, 1, a.shape[-1])

    h, hb = _ln_in(x, meta_tokens.astype(x.dtype), ln_in_g.reshape(1, d), ln_in_b.reshape(1, d), tp)

    tm_e = EXPERT_TILE
    max_tiles = -(-(batch * n_tok) // tm_e) + N_CLASSES
    spare_rows = -(-(batch * (tp - n_tok)) // tm_e) * tm_e
    tn = _row_tile(n, 512)
    xs_rows, row_w = max_tiles * tm_e + spare_rows, d // 2 + LANES
    xs = jnp.zeros((xs_rows // SUBLANES, SUBLANES, row_w), jnp.int32)
    for l in range(depth):
        q_a, k_a, v_a, q_b, k_b, v_b = _in_proj(
            hb, w_in_p, wq_p, wkv_p, row3(g_q_lora), row3(g_kv_lora), row3(g_qk_q), row3(g_qk_k),
            tabs, l, batch, tp)
        o_a, (wg_b, wu_b) = _attention(q_a, k_a, v_a, n_tok, [(wg_rows, (l, depth)), (wu_rows, (l, depth))])
        o_b, (wd_b,) = _attention(q_b, k_b, v_b, n_tok, [(wd_rows, (l, depth))])
        wg_b, wu_b = wg_b.reshape(n_exp, d, ff), wu_b.reshape(n_exp, d, ff)
        wd_b = wd_b.reshape(n_exp, ff, d)
        h1, h1p, ri, counts = _out_proj(
            o_a, o_b, h, w_out_b, row3(g_out_mla), row3(g_out_gqa), row3(ln1_g), row3(ln1_b),
            wr_hl, rbias, l, alpha, tp, n_tok)
        tile_e, tmeta, pos_scatter, pos_gather = _dispatch_plan(ri, counts, max_tiles, tp, n_tok)
        xs = _dispatch(pos_scatter.reshape(n // tn, 1, tn), h1p, xs)
        ys = _experts(tile_e, tmeta, xs.reshape(xs_rows, row_w), wg_b, wu_b, wd_b, max_tiles)
        ys = ys.reshape(ys.shape[0] // SUBLANES, SUBLANES, d // 2)
        if l + 1 < depth:
            h, hb = _combine(pos_gather, ys, h1, row3(ln2_g), row3(ln2_b), l, alpha, tn)
        else:
            out = _combine_final(pos_gather, ys, h1, row3(ln2_g), row3(ln2_b), l, alpha, batch, seq, tp)
    return out
```

```python
import functools
import math

import jax
import jax.numpy as jnp
import numpy as np
from jax import lax
from jax.experimental import pallas as pl
from jax.experimental.pallas import tpu as pltpu

N_META = 16
GRID_W = 64
ROPE_THETA = 10000.0
EPS = 1e-6

MLA_HEADS = 8
Q_LORA = 512
KV_LORA = 256
NOPE_DIM = 128
ROPE_DIM = 64
V_DIM = 128
QK_HEAD = NOPE_DIM + ROPE_DIM
MLA_WIDTH = MLA_HEADS * V_DIM
LOG2_E = math.log2(math.e)
MLA_SCALE = LOG2_E / math.sqrt(QK_HEAD)
MLA_QK_PAD = 256

GQA_HEADS = 8
GQA_KV_HEADS = 2
GQA_HEAD_DIM = 128
GQA_WIDTH = GQA_HEADS * GQA_HEAD_DIM
GQA_SCALE = LOG2_E / math.sqrt(GQA_HEAD_DIM)

N_EXPERTS = 16
N_GROUPS = 4
EXPERTS_PER_GROUP = 4
PAIRS_PER_GROUP = 6
N_CLASSES = N_GROUPS * PAIRS_PER_GROUP
EXPERT_FF = 1024

LANES = 128
VMEM_LIMIT_BYTES = 58 * 1024 * 1024
SUBLANES = 8
BF16_ROWS = 16
EXPERT_TILE = 256
IN_PROJ_ROWS = 320
ATTN_Q_ROWS = 320
OUT_PROJ_SUB_ROWS = 256
ROW_DMA_ROWS = 512
NEG_BIG = -1e30
HI16 = -65536

C_CQ = 0
C_CKV = C_CQ + Q_LORA
C_KR = C_CKV + KV_LORA
C_GQ = C_KR + LANES
C_GK = C_GQ + GQA_WIDTH
C_GV = C_GK + GQA_KV_HEADS * GQA_HEAD_DIM
IN_COLS_PAD = C_GV + GQA_KV_HEADS * GQA_HEAD_DIM


def _cparams(semantics):
    return pltpu.CompilerParams(dimension_semantics=semantics, vmem_limit_bytes=VMEM_LIMIT_BYTES)


def _const_spec(block_shape, index_map):
    return pl.BlockSpec(block_shape, index_map, pipeline_mode=pl.Buffered(1))


def _row_tile(n, cap, mult=BF16_ROWS):
    best = mult
    for t in range(mult, cap + 1, mult):
        if n % t == 0:
            best = t
    assert n % best == 0
    return best


def _layer_norm_rows(z, g, b):
    mu = jnp.mean(z, axis=-1, keepdims=True)
    zc = z - mu
    var = jnp.mean(zc * zc, axis=-1, keepdims=True)
    return zc * lax.rsqrt(var + EPS) * g + b


def _rms_rows(z, g):
    return z * lax.rsqrt(jnp.mean(z * z, axis=-1, keepdims=True) + EPS) * g


def _dot(a, b):
    return jnp.dot(a, b, preferred_element_type=jnp.float32)


def _pack_bf16_pair(lo_f32, hi_f32):
    lo_bits = lax.bitcast_convert_type(lo_f32.astype(jnp.bfloat16).astype(jnp.float32), jnp.int32)
    hi_bits = lax.bitcast_convert_type(hi_f32.astype(jnp.bfloat16).astype(jnp.float32), jnp.int32)
    return lax.shift_right_logical(lo_bits, 16) | (hi_bits & HI16)


def _unpack_bf16_pair(words, dtype=jnp.bfloat16):
    lo = lax.bitcast_convert_type(words << 16, jnp.float32).astype(dtype)
    hi = lax.bitcast_convert_type(words & HI16, jnp.float32).astype(dtype)
    return lo, hi


def _ln_in_kernel(x_ref, meta_ref, g_ref, b_ref, h_ref, hb_ref):
    j = pl.program_id(1)
    last = pl.num_programs(1) - 1

    def emit(rows):
        y = _layer_norm_rows(rows, g_ref[...], b_ref[...])
        h_ref[...] = y
        hb_ref[...] = y.astype(jnp.bfloat16)

    @pl.when(j < last)
    def _():
        emit(x_ref[0])

    @pl.when(j == last)
    def _():
        meta = meta_ref[...]
        pad = jnp.zeros((h_ref.shape[0] - meta.shape[0], meta.shape[1]), meta.dtype)
        emit(jnp.concatenate([meta, pad], axis=0))


def _ln_in(x, meta_tokens, g, b, tp):
    batch, seq, d = x.shape
    tm = tp - seq
    assert seq % tm == 0 and meta_tokens.shape[0] <= tm
    nt = tp // tm
    n = batch * tp
    omap = lambda bi, j: (bi * nt + j, 0)
    cmap = lambda bi, j: (0, 0)
    return pl.pallas_call(
        _ln_in_kernel,
        grid=(batch, nt),
        in_specs=[
            pl.BlockSpec((1, tm, d), lambda bi, j: (bi, jnp.minimum(j, nt - 2), 0)),
            pl.BlockSpec(meta_tokens.shape, cmap),
            pl.BlockSpec((1, d), cmap),
            pl.BlockSpec((1, d), cmap),
        ],
        out_specs=[pl.BlockSpec((tm, d), omap), pl.BlockSpec((tm, d), omap)],
        out_shape=[jax.ShapeDtypeStruct((n, d), jnp.float32), jax.ShapeDtypeStruct((n, d), jnp.bfloat16)],
        compiler_params=_cparams(("parallel", "parallel")),
        name="ln_in",
    )(x, meta_tokens, g, b)


def _swap_halves_64(x):
    lane = lax.broadcasted_iota(jnp.int32, x.shape, 1)
    fwd = pltpu.roll(x, 32, 1)
    bwd = pltpu.roll(x, 96, 1)
    return jnp.where((lane & 63) < 32, bwd, fwd)


def _in_proj_kernel(x_ref, w_in_ref, wq_ref, wkv_ref, gq_ref, gkv_ref, gqq_ref, gqk_ref,
                    cos_a_ref, sin_a_ref, cos_b_ref, sin_b_ref,
                    q_mla_ref, k_mla_ref, v_mla_ref, q_gqa_ref, k_gqa_ref, v_gqa_ref):
    bf16 = jnp.bfloat16
    u = _dot(x_ref[...], w_in_ref[0])
    cos_a, sin_a = cos_a_ref[...], sin_a_ref[...]
    cos_b, sin_b = cos_b_ref[...], sin_b_ref[...]
    lane = lax.broadcasted_iota(jnp.int32, cos_a.shape, 1)

    c_q = _rms_rows(u[:, C_CQ:C_CKV], gq_ref[0]).astype(bf16)
    qa = _dot(c_q, wq_ref[0])
    c_kv = _rms_rows(u[:, C_CKV:C_KR], gkv_ref[0]).astype(bf16)
    kv = _dot(c_kv, wkv_ref[0])
    kr = u[:, C_KR:C_GQ]
    kr = kr * cos_a + _swap_halves_64(kr) * sin_a
    rope_base = MLA_HEADS * NOPE_DIM
    for pair in range(MLA_HEADS // 2):
        blk = qa[:, rope_base + LANES * pair:rope_base + LANES * (pair + 1)]
        rot = blk * cos_a + _swap_halves_64(blk) * sin_a
        for half in range(2):
            h = 2 * pair + half
            own = jnp.where((lane >> 6) == half, rot, 0.0)
            qh = jnp.concatenate([qa[:, NOPE_DIM * h:NOPE_DIM * (h + 1)], own], axis=1) * MLA_SCALE
            q_mla_ref[0, h] = qh.astype(bf16)
    for h in range(MLA_HEADS):
        kh = jnp.concatenate([kv[:, NOPE_DIM * h:NOPE_DIM * (h + 1)], kr], axis=1)
        k_mla_ref[0, h] = kh.astype(bf16)
        v0 = MLA_HEADS * NOPE_DIM + V_DIM * h
        v_mla_ref[0, h] = kv[:, v0:v0 + V_DIM].astype(bf16)

    for h in range(GQA_HEADS):
        qh = _rms_rows(u[:, C_GQ + GQA_HEAD_DIM * h:C_GQ + GQA_HEAD_DIM * (h + 1)], gqq_ref[0])
        qh = qh * cos_b + pltpu.roll(qh, 64, 1) * sin_b
        q_gqa_ref[0, h] = (qh * GQA_SCALE).astype(bf16)
    for h in range(GQA_KV_HEADS):
        kh = _rms_rows(u[:, C_GK + GQA_HEAD_DIM * h:C_GK + GQA_HEAD_DIM * (h + 1)], gqk_ref[0])
        kh = kh * cos_b + pltpu.roll(kh, 64, 1) * sin_b
        k_gqa_ref[0, h] = kh.astype(bf16)
        v_gqa_ref[0, h] = u[:, C_GV + GQA_HEAD_DIM * h:C_GV + GQA_HEAD_DIM * (h + 1)].astype(bf16)


def _in_proj(hb, w_in_p, wq_p, wkv_p, g_q, g_kv, g_qq, g_qk, tabs, layer, batch, tp):
    n, d = hb.shape
    tm = _row_tile(tp, IN_PROJ_ROWS)
    nt = tp // tm
    cos_a, sin_a, cos_b, sin_b = tabs
    wmap = lambda b, i: (layer, 0, 0)
    tmap = lambda b, i: (i, 0)
    omap = lambda b, i: (b, 0, i, 0)
    bf16 = jnp.bfloat16
    return pl.pallas_call(
        _in_proj_kernel,
        grid=(batch, nt),
        in_specs=[
            pl.BlockSpec((tm, d), lambda b, i: (b * nt + i, 0)),
            _const_spec((1, d, IN_COLS_PAD), wmap),
            _const_spec((1, Q_LORA, wq_p.shape[2]), wmap),
            _const_spec((1, KV_LORA, wkv_p.shape[2]), wmap),
            _const_spec((1, 1, Q_LORA), wmap),
            _const_spec((1, 1, KV_LORA), wmap),
            _const_spec((1, 1, GQA_HEAD_DIM), wmap),
            _const_spec((1, 1, GQA_HEAD_DIM), wmap),
            pl.BlockSpec((tm, LANES), tmap),
            pl.BlockSpec((tm, LANES), tmap),
            pl.BlockSpec((tm, LANES), tmap),
            pl.BlockSpec((tm, LANES), tmap),
        ],
        out_specs=[
            pl.BlockSpec((1, MLA_HEADS, tm, MLA_QK_PAD), omap),
            pl.BlockSpec((1, MLA_HEADS, tm, MLA_QK_PAD), omap),
            pl.BlockSpec((1, MLA_HEADS, tm, V_DIM), omap),
            pl.BlockSpec((1, GQA_HEADS, tm, GQA_HEAD_DIM), omap),
            pl.BlockSpec((1, GQA_KV_HEADS, tm, GQA_HEAD_DIM), omap),
            pl.BlockSpec((1, GQA_KV_HEADS, tm, GQA_HEAD_DIM), omap),
        ],
        out_shape=[
            jax.ShapeDtypeStruct((batch, MLA_HEADS, tp, MLA_QK_PAD), bf16),
            jax.ShapeDtypeStruct((batch, MLA_HEADS, tp, MLA_QK_PAD), bf16),
            jax.ShapeDtypeStruct((batch, MLA_HEADS, tp, V_DIM), bf16),
            jax.ShapeDtypeStruct((batch, GQA_HEADS, tp, GQA_HEAD_DIM), bf16),
            jax.ShapeDtypeStruct((batch, GQA_KV_HEADS, tp, GQA_HEAD_DIM), bf16),
            jax.ShapeDtypeStruct((batch, GQA_KV_HEADS, tp, GQA_HEAD_DIM), bf16),
        ],
        compiler_params=_cparams(("parallel", "parallel")),
        name="in_proj",
    )(hb, w_in_p, wq_p, wkv_p, g_q, g_kv, g_qq, g_qk, cos_a, sin_a, cos_b, sin_b)


def _attn_kernel(q_ref, k_ref, v_ref, *rest, tq, n_keys, n_cast):
    o_ref = rest[n_cast]
    for src_ref, dst_ref in zip(rest[:n_cast], rest[n_cast + 1:]):
        dst_ref[...] = src_ref[...].astype(dst_ref.dtype)
    tp = k_ref.shape[2]
    body = tp - LANES
    nt = (((1,), (1,)), ((), ()))
    k_body, k_tail = k_ref[0, 0, :body, :], k_ref[0, 0, body:, :]
    v_body, v_tail = v_ref[0, 0, :body, :], v_ref[0, 0, body:, :]
    tail_ok = lax.broadcasted_iota(jnp.int32, (1, LANES), 1) < (n_keys - body)
    n_q = -(-n_keys // BF16_ROWS) * BF16_ROWS
    if n_q < tp:
        o_ref[n_q:, :] = jnp.zeros((tp - n_q, o_ref.shape[1]), o_ref.dtype)
    for r0 in range(0, n_q, tq):
        rows = slice(r0, min(r0 + tq, n_q))
        q = q_ref[0, 0, rows, :]
        s_body = lax.dot_general(q, k_body, nt, preferred_element_type=jnp.float32)
        s_tail = lax.dot_general(q, k_tail, nt, preferred_element_type=jnp.float32)
        s_tail = jnp.where(tail_ok, s_tail, NEG_BIG)
        m = jnp.maximum(jnp.max(s_body, axis=-1, keepdims=True), jnp.max(s_tail, axis=-1, keepdims=True))
        p_body = jnp.exp2(s_body - m)
        p_tail = jnp.exp2(s_tail - m)
        l = jnp.sum(p_body, axis=-1, keepdims=True) + jnp.sum(p_tail, axis=-1, keepdims=True)
        o = _dot(p_body.astype(jnp.bfloat16), v_body) + _dot(p_tail.astype(jnp.bfloat16), v_tail)
        o_ref[rows, :] = (o / l).astype(o_ref.dtype)


def _attention(q, k, v, n_keys, casts):
    batch, hq, tp, dk = q.shape
    hk, dv = k.shape[1], v.shape[3]
    rep = hq // hk
    tq = _row_tile(tp, ATTN_Q_ROWS)
    steps = batch * hq
    cast_in, cast_out, cast_shapes = [], [], []
    for w, layer in casts:
        rows = w.shape[0] // (layer[1])
        assert rows % steps == 0
        blk = rows // steps
        off = layer[0] * steps
        cast_in.append(pl.BlockSpec((blk, w.shape[1]), lambda b, h, off=off: (off + b * hq + h, 0)))
        cast_out.append(pl.BlockSpec((blk, w.shape[1]), lambda b, h: (b * hq + h, 0)))
        cast_shapes.append(jax.ShapeDtypeStruct((rows, w.shape[1]), jnp.bfloat16))
    outs = pl.pallas_call(
        functools.partial(_attn_kernel, tq=tq, n_keys=n_keys, n_cast=len(casts)),
        grid=(batch, hq),
        in_specs=[
            pl.BlockSpec((1, 1, tp, dk), lambda b, h: (b, h, 0, 0)),
            pl.BlockSpec((1, 1, tp, dk), lambda b, h: (b, h // rep, 0, 0)),
            pl.BlockSpec((1, 1, tp, dv), lambda b, h: (b, h // rep, 0, 0)),
        ] + cast_in,
        out_specs=[pl.BlockSpec((tp, dv), lambda b, h: (b, h))] + cast_out,
        out_shape=[jax.ShapeDtypeStruct((batch * tp, hq * dv), jnp.bfloat16)] + cast_shapes,
        compiler_params=_cparams(("parallel", "parallel")),
        name="attention",
    )(q, k, v, *[w for w, _ in casts])
    return outs[0], outs[1:]


def _first_index_of_max(vals, lane_f):
    m = jnp.max(vals, axis=-1, keepdims=True)
    idx = jnp.min(jnp.where(vals == m, lane_f, float(LANES)), axis=-1, keepdims=True)
    return m, idx


def _out_proj_kernel(oa_ref, ob_ref, h_ref, w_out_ref, ga_ref, gb_ref, lg_ref, lb_ref,
                     wr_hl_ref, rbias_ref,
                     h1_ref, h1p_ref, ri_ref, cnt_ref, carry_ref, *, alpha, tp, n_tok, sub):
    i = pl.program_id(0)
    tm = h_ref.shape[0]

    @pl.when(i == 0)
    def _():
        carry_ref[...] = jnp.zeros_like(carry_ref)

    blocks = [slice(r0, r0 + sub) for r0 in range(0, tm, sub)]
    mixed = [_out_proj_matmul(rows, oa_ref, ob_ref, w_out_ref, ga_ref, gb_ref) for rows in blocks]
    routed_in = [_out_proj_norm(rows, m, h_ref, lg_ref, lb_ref, wr_hl_ref, h1_ref, alpha=alpha)
                 for rows, m in zip(blocks, mixed)]
    for rows, (h1, logits) in zip(blocks, routed_in):
        _out_proj_route(rows, i * tm + rows.start, h1, logits, rbias_ref, h1p_ref, ri_ref, carry_ref,
                        tp=tp, n_tok=n_tok)
    cnt_ref[...] = carry_ref[...]


def _out_proj_matmul(rows, oa_ref, ob_ref, w_out_ref, ga_ref, gb_ref):
    bf16 = jnp.bfloat16
    na = _rms_rows(oa_ref[rows, :].astype(jnp.float32), ga_ref[0]).astype(bf16)
    nb = _rms_rows(ob_ref[rows, :].astype(jnp.float32), gb_ref[0]).astype(bf16)
    return _dot(na, w_out_ref[0, :MLA_WIDTH, :]) + _dot(nb, w_out_ref[0, MLA_WIDTH:, :])


def _out_proj_norm(rows, mixed, h_ref, lg_ref, lb_ref, wr_hl_ref, h1_ref, *, alpha):
    bf16 = jnp.bfloat16
    h1 = _layer_norm_rows(alpha * h_ref[rows, :] + mixed, lg_ref[0], lb_ref[0])
    h1_ref[rows, :] = h1
    hi = h1.astype(bf16)
    lo = (h1 - hi.astype(jnp.float32)).astype(bf16)
    hi_prod = _dot(hi, wr_hl_ref[...])
    logits = hi_prod[:, :LANES] + hi_prod[:, LANES:] + _dot(lo, wr_hl_ref[:, :LANES])
    return h1, logits


def _out_proj_route(rows, row0, h1, logits, rbias_ref, h1p_ref, ri_ref, carry_ref, *, tp, n_tok):
    bf16 = jnp.bfloat16
    f32 = jnp.float32
    tm = rows.stop - rows.start
    half = h1.shape[1] // 2
    scores = jax.nn.sigmoid(logits)

    lane = lax.broadcasted_iota(jnp.int32, (tm, LANES), 1)
    lane_f = lane.astype(f32)
    neg = -jnp.inf
    sel = jnp.where(lane < N_EXPERTS, scores + rbias_ref[...], neg)
    grp = lane >> 2

    best = None
    for g in range(N_GROUPS):
        mg = jnp.where(grp == g, sel, neg)
        m1, i1 = _first_index_of_max(mg, lane_f)
        m2 = jnp.max(jnp.where(lane_f == i1, neg, mg), axis=-1, keepdims=True)
        gs = m1 + m2
        if best is None:
            best, gi = gs, jnp.zeros_like(gs)
        else:
            better = gs > best
            gi = jnp.where(better, float(g), gi)
            best = jnp.where(better, gs, best)

    mg = jnp.where(grp.astype(f32) == gi, sel, neg)
    _, e1 = _first_index_of_max(mg, lane_f)
    mg2 = jnp.where(lane_f == e1, neg, mg)
    _, e2 = _first_index_of_max(mg2, lane_f)
    w1 = jnp.sum(jnp.where(lane_f == e1, scores, 0.0), axis=-1, keepdims=True)
    w2 = jnp.sum(jnp.where(lane_f == e2, scores, 0.0), axis=-1, keepdims=True)
    den = w1 + w2
    w1, w2 = w1 / den, w2 / den

    first_lower = e1 < e2
    la = jnp.where(first_lower, e1, e2) - EXPERTS_PER_GROUP * gi
    lb = jnp.where(first_lower, e2, e1) - EXPERTS_PER_GROUP * gi
    w_a = jnp.where(first_lower, w1, w2)
    w_b = jnp.where(first_lower, w2, w1)
    cls = PAIRS_PER_GROUP * gi + la * (7.0 - la) * 0.5 + (lb - la - 1.0)

    row = (row0 + lax.broadcasted_iota(jnp.int32, (tm, 1), 0)).astype(f32)
    routed = (row - jnp.floor((row + 0.5) * (1.0 / tp)) * tp) < n_tok
    onehot = jnp.where((lane_f == cls) & routed, 1.0, 0.0)

    r_i = lax.broadcasted_iota(jnp.int32, (tm, tm), 0)
    c_i = lax.broadcasted_iota(jnp.int32, (tm, tm), 1)
    lower = jnp.where(c_i < r_i, 1.0, 0.0).astype(bf16)
    before = _dot(lower, onehot.astype(bf16)) + carry_ref[...]
    rank = jnp.sum(onehot * before, axis=-1, keepdims=True)
    carry_ref[...] += jnp.sum(onehot, axis=0, keepdims=True)

    cls_out = jnp.where(routed, cls, -1.0)
    ri_ref[rows, :] = jnp.where(lane == 0, cls_out, jnp.where(lane == 1, rank, 0.0)).astype(jnp.int32)

    h1p_ref[rows, :half] = _pack_bf16_pair(h1[:, :half], h1[:, half:])
    weights = jnp.where(lane == 0, w_a, jnp.where(lane == 1, w_b, 0.0))
    h1p_ref[rows, half:] = lax.bitcast_convert_type(weights, jnp.int32)


def _out_proj(o_a, o_b, h, w_out_b, g_a, g_b, ln_g, ln_b, wr_hl, rbias, layer, alpha, tp, n_tok):
    n, d = h.shape
    sub = _row_tile(n, OUT_PROJ_SUB_ROWS)
    tm = 2 * sub if n % (2 * sub) == 0 else sub
    row_w = d // 2 + LANES
    wmap = lambda i: (layer, 0, 0)
    cmap = lambda i: (0, 0)
    rmap = lambda i: (i, 0)
    return pl.pallas_call(
        functools.partial(_out_proj_kernel, alpha=alpha, tp=tp, n_tok=n_tok, sub=sub),
        grid=(n // tm,),
        in_specs=[
            pl.BlockSpec((tm, MLA_WIDTH), rmap),
            pl.BlockSpec((tm, GQA_WIDTH), rmap),
            pl.BlockSpec((tm, d), rmap),
            _const_spec((1, MLA_WIDTH + GQA_WIDTH, d), wmap),
            _const_spec((1, 1, MLA_WIDTH), wmap),
            _const_spec((1, 1, GQA_WIDTH), wmap),
            _const_spec((1, 1, d), wmap),
            _const_spec((1, 1, d), wmap),
            _const_spec((d, 2 * LANES), cmap),
            _const_spec((1, LANES), cmap),
        ],
        out_specs=[
            pl.BlockSpec((tm, d), rmap),
            pl.BlockSpec((tm, row_w), rmap),
            pl.BlockSpec((tm, LANES), rmap),
            pl.BlockSpec((1, LANES), cmap),
        ],
        out_shape=[
            jax.ShapeDtypeStruct((n, d), jnp.float32),
            jax.ShapeDtypeStruct((n, row_w), jnp.int32),
            jax.ShapeDtypeStruct((n, LANES), jnp.int32),
            jax.ShapeDtypeStruct((1, LANES), jnp.float32),
        ],
        scratch_shapes=[pltpu.VMEM((1, LANES), jnp.float32)],
        compiler_params=_cparams(("arbitrary",)),
        name="out_proj",
    )(o_a, o_b, h, w_out_b, g_a, g_b, ln_g, ln_b, wr_hl, rbias)


def _start_row_copies(idx_ref, n_rows, make_copy):
    def issue(g, carry):
        for k in range(SUBLANES):
            idx = idx_ref[0, 0, g * SUBLANES + k]
            make_copy(g, k, lax.shift_right_logical(idx, 3), idx & (SUBLANES - 1)).start()
        return carry

    lax.fori_loop(0, n_rows // SUBLANES, issue, 0)


def _dispatch_kernel(pos_ref, x_ref, xs_init_hbm, xs_hbm, buf_ref, sem):
    del xs_init_hbm
    i = pl.program_id(0)
    last = pl.num_programs(0) - 1
    slot = i % 2
    tn = x_ref.shape[0]
    groups = tn // SUBLANES

    def wait_slot(s):
        pltpu.make_async_copy(buf_ref.at[s], xs_hbm.at[pl.ds(0, groups)], sem.at[s]).wait()

    @pl.when(i >= 2)
    def _():
        wait_slot(slot)

    buf_ref[slot] = x_ref[...].reshape(groups, SUBLANES, x_ref.shape[1])
    _start_row_copies(pos_ref, tn, lambda g, k, hi, lo: pltpu.make_async_copy(
        buf_ref.at[slot, g, pl.ds(k, 1), :], xs_hbm.at[hi, pl.ds(lo, 1), :], sem.at[slot]))

    @pl.when(i == last)
    def _():
        wait_slot(slot)

        @pl.when(last >= 1)
        def _():
            wait_slot(1 - slot)


def _dispatch(pos3, h1p, xs_init):
    n, row_w = h1p.shape
    tn = pos3.shape[2]
    return pl.pallas_call(
        _dispatch_kernel,
        grid=(n // tn,),
        in_specs=[
            pl.BlockSpec((1, 1, tn), lambda i: (i, 0, 0), memory_space=pltpu.SMEM),
            pl.BlockSpec((tn, row_w), lambda i: (i, 0)),
            pl.BlockSpec(memory_space=pl.ANY),
        ],
        out_specs=pl.BlockSpec(memory_space=pl.ANY),
        out_shape=jax.ShapeDtypeStruct(xs_init.shape, xs_init.dtype),
        scratch_shapes=[pltpu.VMEM((2, tn // SUBLANES, SUBLANES, row_w), h1p.dtype),
                        pltpu.SemaphoreType.DMA((2,))],
        input_output_aliases={2: 0},
        compiler_params=_cparams(("arbitrary",)),
        name="dispatch",
    )(pos3, h1p, xs_init)


def _experts_kernel(plan_ref, meta_ref, xs_ref, wg_hbm, wu_hbm, wd_hbm, ys_ref, acc_ref, wg_buf, wu_buf, wd_buf,
                    sem):
    i = pl.program_id(0)
    k = pl.program_id(1)
    steps = 2 * pl.num_programs(0)
    t = 2 * i + k
    half = wg_buf.shape[1] // 2

    def weight_copies(e, s):
        return (pltpu.make_async_copy(wg_hbm.at[e], wg_buf.at[s], sem.at[s]),
                pltpu.make_async_copy(wu_hbm.at[e], wu_buf.at[s], sem.at[s]),
                pltpu.make_async_copy(wd_hbm.at[e], wd_buf.at[s], sem.at[s]))

    @pl.when(i < meta_ref[0])
    def _():
        slot = plan_ref[2 * steps + t]

        @pl.when(plan_ref[steps + t] == 1)
        def _():
            @pl.when(t == 0)
            def _():
                for c in weight_copies(plan_ref[0], slot):
                    c.start()

            for c in weight_copies(plan_ref[t], slot):
                c.wait()
            nxt = plan_ref[3 * steps + t]

            @pl.when(nxt >= 0)
            def _():
                for c in weight_copies(nxt, 1 - slot):
                    c.start()

        lo, hi = _unpack_bf16_pair(xs_ref[:, :half])
        g = _dot(lo, wg_buf[slot, :half, :]) + _dot(hi, wg_buf[slot, half:, :])
        u = _dot(lo, wu_buf[slot, :half, :]) + _dot(hi, wu_buf[slot, half:, :])
        a = (g * jax.nn.sigmoid(g) * u).astype(jnp.bfloat16)
        y = _dot(a, wd_buf[slot])
        second = (k + i) % 2 == 1
        wab = lax.bitcast_convert_type(xs_ref[:, half:], jnp.float32)
        y = y * jnp.where(second, wab[:, 1:2], wab[:, 0:1])

        @pl.when(k == 0)
        def _():
            acc_ref[...] = y

        @pl.when(k == 1)
        def _():
            total = acc_ref[...] + y
            ys_ref[...] = _pack_bf16_pair(total[:, :half], total[:, half:])

    @pl.when((i >= meta_ref[0]) & (k == 1))
    def _():
        ys_ref[...] = jnp.zeros_like(ys_ref)


def _experts(plan, meta, xs, wg_b, wu_b, wd_b, max_tiles):
    tm = EXPERT_TILE
    row_w = xs.shape[1]
    d, ff = wg_b.shape[1], wg_b.shape[2]

    def xmap(i, k, pr, mt):
        return (jnp.minimum(i, mt[0] - 1), 0)

    grid_spec = pltpu.PrefetchScalarGridSpec(
        num_scalar_prefetch=2,
        grid=(max_tiles, 2),
        in_specs=[
            pl.BlockSpec((tm, row_w), xmap),
            pl.BlockSpec(memory_space=pl.ANY),
            pl.BlockSpec(memory_space=pl.ANY),
            pl.BlockSpec(memory_space=pl.ANY),
        ],
        out_specs=pl.BlockSpec((tm, d // 2), lambda i, k, pr, mt: (i, 0)),
        scratch_shapes=[
            pltpu.VMEM((tm, d), jnp.float32),
            pltpu.VMEM((2, d, ff), wg_b.dtype),
            pltpu.VMEM((2, d, ff), wu_b.dtype),
            pltpu.VMEM((2, ff, d), wd_b.dtype),
            pltpu.SemaphoreType.DMA((2,)),
        ],
    )
    return pl.pallas_call(
        _experts_kernel,
        grid_spec=grid_spec,
        out_shape=jax.ShapeDtypeStruct((max_tiles * tm, d // 2), jnp.int32),
        compiler_params=_cparams(("arbitrary", "arbitrary")),
        name="experts",
    )(plan, meta, xs, wg_b, wu_b, wd_b)


def _combine_kernel(pos_ref, pos_next_ref, ys_hbm, h1_ref, g_ref, b_ref, *rest, alpha, final):
    if final:
        out_ref, ybuf_ref, sem = rest
        i = pl.program_id(0) * pl.num_programs(1) + pl.program_id(1)
        last = pl.num_programs(0) * pl.num_programs(1) - 1
    else:
        h_ref, hb_ref, ybuf_ref, sem = rest
        i = pl.program_id(0)
        last = pl.num_programs(0) - 1
    slot = i % 2
    tn = h1_ref.shape[0]

    def start_gather(idx_ref, s):
        _start_row_copies(idx_ref, tn, lambda g, k, hi, lo: pltpu.make_async_copy(
            ys_hbm.at[hi, pl.ds(lo, 1), :], ybuf_ref.at[s, g, pl.ds(k, 1), :], sem.at[s]))

    @pl.when(i == 0)
    def _():
        start_gather(pos_ref, 0)

    pltpu.make_async_copy(ys_hbm.at[pl.ds(0, tn // SUBLANES)], ybuf_ref.at[slot], sem.at[slot]).wait()

    @pl.when(i < last)
    def _():
        start_gather(pos_next_ref, 1 - slot)

    lo, hi = _unpack_bf16_pair(ybuf_ref[slot].reshape(tn, h1_ref.shape[1] // 2), jnp.float32)
    y = _layer_norm_rows(alpha * h1_ref[...] + jnp.concatenate([lo, hi], axis=1), g_ref[0], b_ref[0])
    if final:
        out_ref[0] = y
    else:
        h_ref[...] = y
        hb_ref[...] = y.astype(jnp.bfloat16)


def _combine_final(pos, ys, h1, ln_g, ln_b, layer, alpha, batch, seq, tp):
    n, d = h1.shape
    tn = tp - seq
    per_batch = tp // tn
    nx = seq // tn
    pos3 = pos.reshape(batch * per_batch, 1, tn)
    wmap = lambda b, j: (layer, 0, 0)
    cur = lambda b, j: b * per_batch + j

    def nxt(b, j):
        wrap = j + 1 >= nx
        return jnp.where(wrap, jnp.minimum(b + 1, batch - 1) * per_batch, b * per_batch + j + 1)

    return pl.pallas_call(
        functools.partial(_combine_kernel, alpha=alpha, final=True),
        grid=(batch, nx),
        in_specs=[
            pl.BlockSpec((1, 1, tn), lambda b, j: (cur(b, j), 0, 0), memory_space=pltpu.SMEM),
            pl.BlockSpec((1, 1, tn), lambda b, j: (nxt(b, j), 0, 0), memory_space=pltpu.SMEM),
            pl.BlockSpec(memory_space=pl.ANY),
            pl.BlockSpec((tn, d), lambda b, j: (cur(b, j), 0)),
            _const_spec((1, 1, d), wmap),
            _const_spec((1, 1, d), wmap),
        ],
        out_specs=pl.BlockSpec((1, tn, d), lambda b, j: (b, j, 0)),
        out_shape=jax.ShapeDtypeStruct((batch, seq, d), jnp.float32),
        scratch_shapes=[pltpu.VMEM((2, tn // SUBLANES, SUBLANES, d // 2), jnp.int32),
                        pltpu.SemaphoreType.DMA((2,))],
        compiler_params=_cparams(("arbitrary", "arbitrary")),
        name="combine_final",
    )(pos3, pos3, ys, h1, ln_g, ln_b)


def _combine(pos, ys, h1, ln_g, ln_b, layer, alpha, tn):
    n, d = h1.shape
    steps = n // tn
    pos3 = pos.reshape(steps, 1, tn)
    wmap = lambda i: (layer, 0, 0)
    rmap = lambda i: (i, 0)
    return pl.pallas_call(
        functools.partial(_combine_kernel, alpha=alpha, final=False),
        grid=(steps,),
        in_specs=[
            pl.BlockSpec((1, 1, tn), lambda i: (i, 0, 0), memory_space=pltpu.SMEM),
            pl.BlockSpec((1, 1, tn), lambda i: (jnp.minimum(i + 1, steps - 1), 0, 0), memory_space=pltpu.SMEM),
            pl.BlockSpec(memory_space=pl.ANY),
            pl.BlockSpec((tn, d), rmap),
            _const_spec((1, 1, d), wmap),
            _const_spec((1, 1, d), wmap),
        ],
        out_specs=[pl.BlockSpec((tn, d), rmap), pl.BlockSpec((tn, d), rmap)],
        out_shape=[jax.ShapeDtypeStruct((n, d), jnp.float32), jax.ShapeDtypeStruct((n, d), jnp.bfloat16)],
        scratch_shapes=[pltpu.VMEM((2, tn // SUBLANES, SUBLANES, d // 2), jnp.int32),
                        pltpu.SemaphoreType.DMA((2,))],
        compiler_params=_cparams(("arbitrary",)),
        name="combine",
    )(pos3, pos3, ys, h1, ln_g, ln_b)


def _rope_tables(seq, tp):
    rows = seq // GRID_W
    pad = tp - seq - N_META
    pos_row = jnp.concatenate([jnp.repeat(jnp.arange(rows, dtype=jnp.float32), GRID_W),
                               jnp.full((N_META,), -1.0, jnp.float32), jnp.zeros((pad,), jnp.float32)])
    pos_col = jnp.concatenate([jnp.tile(jnp.arange(GRID_W, dtype=jnp.float32), rows),
                               jnp.arange(N_META, dtype=jnp.float32), jnp.zeros((pad,), jnp.float32)])

    def tables(rot_dim):
        n = rot_dim // 4
        inv = ROPE_THETA ** (-jnp.arange(n, dtype=jnp.float32) / n)
        ang = jnp.concatenate([pos_row[:, None] * inv, pos_col[:, None] * inv], axis=-1)
        cos, sin = jnp.cos(ang), jnp.sin(ang)
        reps = LANES // rot_dim
        return (jnp.tile(jnp.concatenate([cos, cos], axis=-1), (1, reps)),
                jnp.tile(jnp.concatenate([-sin, sin], axis=-1), (1, reps)))

    cos_a, sin_a = tables(ROPE_DIM)
    cos_b, sin_b = tables(GQA_HEAD_DIM)
    return cos_a, sin_a, cos_b, sin_b


def _relayout_weights(w_in, w_q_b, w_kv_b):
    bf16 = jnp.bfloat16
    off_kr = Q_LORA + KV_LORA
    off_gq = off_kr + ROPE_DIM
    kr = w_in[:, :, off_kr:off_gq]
    w_in_p = jnp.concatenate([w_in[:, :, :off_kr], kr, kr, w_in[:, :, off_gq:]], axis=-1).astype(bf16)
    depth = w_in.shape[0]
    wq = w_q_b.reshape(depth, Q_LORA, MLA_HEADS, QK_HEAD)
    wq_p = jnp.concatenate([wq[..., :NOPE_DIM].reshape(depth, Q_LORA, -1),
                            wq[..., NOPE_DIM:].reshape(depth, Q_LORA, -1)], axis=-1).astype(bf16)
    wkv = w_kv_b.reshape(depth, KV_LORA, MLA_HEADS, NOPE_DIM + V_DIM)
    wkv_p = jnp.concatenate([wkv[..., :NOPE_DIM].reshape(depth, KV_LORA, -1),
                             wkv[..., NOPE_DIM:].reshape(depth, KV_LORA, -1)], axis=-1).astype(bf16)
    return w_in_p, wq_p, wkv_p


def _dispatch_plan(ri, counts, max_tiles, tp, n_tok):
    tm = EXPERT_TILE
    n = ri.shape[0]
    rows = max_tiles * tm
    cls, rank = ri[:, 0], ri[:, 1]
    cnt = counts[0, :N_CLASSES].astype(jnp.int32)
    tiles_c = (cnt + tm - 1) // tm
    tile_end = jnp.cumsum(tiles_c)
    tile_start = tile_end - tiles_c
    n_tiles = tile_end[-1]
    routed = cls >= 0
    tok = np.arange(n, dtype=np.int32)
    spare = jnp.asarray(rows + (tok // tp) * (tp - n_tok) + (tok % tp - n_tok), dtype=jnp.int32)
    in_class = cls[None, :] == jnp.arange(N_CLASSES, dtype=jnp.int32)[:, None]
    sorted_pos = jnp.sum(jnp.where(in_class, (tile_start * tm)[:, None], 0), axis=0) + rank
    pos_scatter = jnp.where(routed, sorted_pos, spare).astype(jnp.int32)
    pos_gather = jnp.where(routed, sorted_pos, 0).astype(jnp.int32)

    t = jnp.minimum(jnp.arange(max_tiles, dtype=jnp.int32), n_tiles - 1)
    tile_cls = jnp.sum((t[:, None] >= tile_end[None, :]).astype(jnp.int32), axis=1)
    tile_cls = jnp.minimum(tile_cls, N_CLASSES - 1)
    pair_lo = jnp.array([0, 0, 0, 1, 1, 2], jnp.int32)
    pair_hi = jnp.array([1, 2, 3, 2, 3, 3], jnp.int32)
    e_a = EXPERTS_PER_GROUP * (tile_cls // PAIRS_PER_GROUP) + pair_lo[tile_cls % PAIRS_PER_GROUP]
    e_b = EXPERTS_PER_GROUP * (tile_cls // PAIRS_PER_GROUP) + pair_hi[tile_cls % PAIRS_PER_GROUP]
    odd = (jnp.arange(max_tiles) % 2) == 1
    tile_e = jnp.stack([jnp.where(odd, e_b, e_a), jnp.where(odd, e_a, e_b)], axis=-1).reshape(-1)
    step_tile = jnp.arange(2 * max_tiles) // 2
    tile_e = jnp.where(step_tile >= n_tiles, tile_e[2 * n_tiles - 1], tile_e).astype(jnp.int32)
    n_steps = 2 * max_tiles
    step = jnp.arange(n_steps, dtype=jnp.int32)
    used = step_tile < n_tiles
    prev_e = jnp.concatenate([jnp.full((1,), -1, jnp.int32), tile_e[:-1]])
    run_start = used & (tile_e != prev_e)
    slot = jnp.cumsum(run_start.astype(jnp.int32)) % 2
    later_start = run_start[None, :] & (step[None, :] > step[:, None])
    next_start = jnp.min(jnp.where(later_start, step[None, :], n_steps), axis=1)
    next_e = jnp.where(next_start < n_steps, tile_e[jnp.minimum(next_start, n_steps - 1)], -1)
    plan = jnp.concatenate([tile_e, run_start.astype(jnp.int32), slot, next_e]).astype(jnp.int32)
    meta = n_tiles.reshape(1).astype(jnp.int32)
    return plan, meta, pos_scatter, pos_gather


def kernel(x, meta_tokens, ln_in_g, ln_in_b, w_in, g_q_lora, w_q_b, g_kv_lora, w_kv_b, g_qk_q, g_qk_k,
           g_out_mla, g_out_gqa, w_out, ln1_g, ln1_b, w_router, router_bias, w_gate, w_up, w_down,
           ln2_g, ln2_b):
    batch, seq, d = x.shape
    depth = w_in.shape[0]
    n_tok = seq + N_META
    tp = -(-n_tok // LANES) * LANES
    n = batch * tp
    alpha = (2.0 * depth) ** 0.25
    bf16 = jnp.bfloat16
    f32 = jnp.float32

    tabs = _rope_tables(seq, tp)
    w_in_p, wq_p, wkv_p = _relayout_weights(w_in, w_q_b, w_kv_b)
    w_out_b = w_out.astype(bf16)
    n_exp, ff = w_gate.shape[1], w_gate.shape[3]
    wg_rows = w_gate.reshape(depth * n_exp * d, ff)
    wu_rows = w_up.reshape(depth * n_exp * d, ff)
    wd_rows = w_down.reshape(depth * n_exp * ff, d)
    wr = jnp.pad(w_router.astype(f32), ((0, 0), (0, LANES - N_EXPERTS)))
    wr_hi = wr.astype(bf16)
    wr_lo = (wr - wr_hi.astype(f32)).astype(bf16)
    wr_hl = jnp.concatenate([wr_hi, wr_lo], axis=1)
    rbias = jnp.pad(router_bias.astype(f32), (0, LANES - N_EXPERTS)).reshape(1, LANES)
    row3 = lambda a: a.reshape(depth, 1, a.shape[-1])

    h, hb = _ln_in(x, meta_tokens.astype(x.dtype), ln_in_g.reshape(1, d), ln_in_b.reshape(1, d), tp)

    tm_e = EXPERT_TILE
    max_tiles = -(-(batch * n_tok) // tm_e) + N_CLASSES
    spare_rows = -(-(batch * (tp - n_tok)) // tm_e) * tm_e
    tn = _row_tile(n, ROW_DMA_ROWS)
    xs_rows, row_w = max_tiles * tm_e + spare_rows, d // 2 + LANES
    xs = jnp.zeros((xs_rows // SUBLANES, SUBLANES, row_w), jnp.int32)
    for l in range(depth):
        q_a, k_a, v_a, q_b, k_b, v_b = _in_proj(
            hb, w_in_p, wq_p, wkv_p, row3(g_q_lora), row3(g_kv_lora), row3(g_qk_q), row3(g_qk_k),
            tabs, l, batch, tp)
        o_a, (wg_b, wu_b) = _attention(q_a, k_a, v_a, n_tok, [(wg_rows, (l, depth)), (wu_rows, (l, depth))])
        o_b, (wd_b,) = _attention(q_b, k_b, v_b, n_tok, [(wd_rows, (l, depth))])
        wg_b, wu_b = wg_b.reshape(n_exp, d, ff), wu_b.reshape(n_exp, d, ff)
        wd_b = wd_b.reshape(n_exp, ff, d)
        h1, h1p, ri, counts = _out_proj(
            o_a, o_b, h, w_out_b, row3(g_out_mla), row3(g_out_gqa), row3(ln1_g), row3(ln1_b),
            wr_hl, rbias, l, alpha, tp, n_tok)
        tile_e, tmeta, pos_scatter, pos_gather = _dispatch_plan(ri, counts, max_tiles, tp, n_tok)
        xs = _dispatch(pos_scatter.reshape(n // tn, 1, tn), h1p, xs)
        ys = _experts(tile_e, tmeta, xs.reshape(xs_rows, row_w), wg_b, wu_b, wd_b, max_tiles)
        ys = ys.reshape(ys.shape[0] // SUBLANES, SUBLANES, d // 2)
        if l + 1 < depth:
            h, hb = _combine(pos_gather, ys, h1, row3(ln2_g), row3(ln2_b), l, alpha, tn)
        else:
            out = _combine_final(pos_gather, ys, h1, row3(ln2_g), row3(ln2_b), l, alpha, batch, seq, tp)
    return out
```

```python
import functools
import math

import jax
import jax.numpy as jnp
import numpy as np
from jax import lax
from jax.experimental import pallas as pl
from jax.experimental.pallas import tpu as pltpu

N_META = 16
GRID_W = 64
ROPE_THETA = 10000.0
EPS = 1e-6

MLA_HEADS = 8
Q_LORA = 512
KV_LORA = 256
NOPE_DIM = 128
ROPE_DIM = 64
V_DIM = 128
QK_HEAD = NOPE_DIM + ROPE_DIM
MLA_WIDTH = MLA_HEADS * V_DIM
LOG2_E = math.log2(math.e)
MLA_SCALE = LOG2_E / math.sqrt(QK_HEAD)
MLA_QK_PAD = 256

GQA_HEADS = 8
GQA_KV_HEADS = 2
GQA_HEAD_DIM = 128
GQA_WIDTH = GQA_HEADS * GQA_HEAD_DIM
GQA_SCALE = LOG2_E / math.sqrt(GQA_HEAD_DIM)

N_EXPERTS = 16
N_GROUPS = 4
EXPERTS_PER_GROUP = 4
PAIRS_PER_GROUP = 6
N_CLASSES = N_GROUPS * PAIRS_PER_GROUP
EXPERT_FF = 1024

LANES = 128
VMEM_LIMIT_BYTES = 58 * 1024 * 1024
SUBLANES = 8
BF16_ROWS = 16
EXPERT_TILE = 256
IN_PROJ_ROWS = 320
ATTN_Q_ROWS = 320
OUT_PROJ_SUB_ROWS = 256
ROW_DMA_ROWS = 512
NEG_BIG = -1e30
HI16 = -65536

C_CQ = 0
C_CKV = C_CQ + Q_LORA
C_KR = C_CKV + KV_LORA
C_GQ = C_KR + LANES
C_GK = C_GQ + GQA_WIDTH
C_GV = C_GK + GQA_KV_HEADS * GQA_HEAD_DIM
IN_COLS_PAD = C_GV + GQA_KV_HEADS * GQA_HEAD_DIM


def _cparams(semantics):
    return pltpu.CompilerParams(dimension_semantics=semantics, vmem_limit_bytes=VMEM_LIMIT_BYTES)


def _const_spec(block_shape, index_map):
    return pl.BlockSpec(block_shape, index_map, pipeline_mode=pl.Buffered(1))


def _row_tile(n, cap, mult=BF16_ROWS):
    best = mult
    for t in range(mult, cap + 1, mult):
        if n % t == 0:
            best = t
    assert n % best == 0
    return best


def _layer_norm_rows(z, g, b):
    mu = jnp.mean(z, axis=-1, keepdims=True)
    zc = z - mu
    var = jnp.mean(zc * zc, axis=-1, keepdims=True)
    return zc * lax.rsqrt(var + EPS) * g + b


def _rms_rows(z, g):
    return z * lax.rsqrt(jnp.mean(z * z, axis=-1, keepdims=True) + EPS) * g


def _dot(a, b):
    return jnp.dot(a, b, preferred_element_type=jnp.float32)


def _pack_bf16_pair(lo_f32, hi_f32):
    lo_bits = lax.bitcast_convert_type(lo_f32.astype(jnp.bfloat16).astype(jnp.float32), jnp.int32)
    hi_bits = lax.bitcast_convert_type(hi_f32.astype(jnp.bfloat16).astype(jnp.float32), jnp.int32)
    return lax.shift_right_logical(lo_bits, 16) | (hi_bits & HI16)


def _unpack_bf16_pair(words, dtype=jnp.bfloat16):
    lo = lax.bitcast_convert_type(words << 16, jnp.float32).astype(dtype)
    hi = lax.bitcast_convert_type(words & HI16, jnp.float32).astype(dtype)
    return lo, hi


def _ln_in_kernel(x_ref, meta_ref, g_ref, b_ref, h_ref, hb_ref):
    j = pl.program_id(1)
    last = pl.num_programs(1) - 1

    def emit(rows):
        y = _layer_norm_rows(rows, g_ref[...], b_ref[...])
        h_ref[...] = y
        hb_ref[...] = y.astype(jnp.bfloat16)

    @pl.when(j < last)
    def _():
        emit(x_ref[0])

    @pl.when(j == last)
    def _():
        meta = meta_ref[...]
        pad = jnp.zeros((h_ref.shape[0] - meta.shape[0], meta.shape[1]), meta.dtype)
        emit(jnp.concatenate([meta, pad], axis=0))


def _ln_in(x, meta_tokens, g, b, tp):
    batch, seq, d = x.shape
    tm = tp - seq
    assert seq % tm == 0 and meta_tokens.shape[0] <= tm
    nt = tp // tm
    n = batch * tp
    omap = lambda bi, j: (bi * nt + j, 0)
    cmap = lambda bi, j: (0, 0)
    return pl.pallas_call(
        _ln_in_kernel,
        grid=(batch, nt),
        in_specs=[
            pl.BlockSpec((1, tm, d), lambda bi, j: (bi, jnp.minimum(j, nt - 2), 0)),
            pl.BlockSpec(meta_tokens.shape, cmap),
            pl.BlockSpec((1, d), cmap),
            pl.BlockSpec((1, d), cmap),
        ],
        out_specs=[pl.BlockSpec((tm, d), omap), pl.BlockSpec((tm, d), omap)],
        out_shape=[jax.ShapeDtypeStruct((n, d), jnp.float32), jax.ShapeDtypeStruct((n, d), jnp.bfloat16)],
        compiler_params=_cparams(("parallel", "parallel")),
        name="ln_in",
    )(x, meta_tokens, g, b)


def _swap_halves_64(x):
    lane = lax.broadcasted_iota(jnp.int32, x.shape, 1)
    fwd = pltpu.roll(x, 32, 1)
    bwd = pltpu.roll(x, 96, 1)
    return jnp.where((lane & 63) < 32, bwd, fwd)


def _in_proj_kernel(x_ref, w_in_ref, wq_ref, wkv_ref, gq_ref, gkv_ref, gqq_ref, gqk_ref,
                    cos_a_ref, sin_a_ref, cos_b_ref, sin_b_ref,
                    q_mla_ref, k_mla_ref, v_mla_ref, q_gqa_ref, k_gqa_ref, v_gqa_ref):
    bf16 = jnp.bfloat16
    u = _dot(x_ref[...], w_in_ref[0])
    cos_a, sin_a = cos_a_ref[...], sin_a_ref[...]
    cos_b, sin_b = cos_b_ref[...], sin_b_ref[...]
    lane = lax.broadcasted_iota(jnp.int32, cos_a.shape, 1)

    c_q = _rms_rows(u[:, C_CQ:C_CKV], gq_ref[0]).astype(bf16)
    qa = _dot(c_q, wq_ref[0])
    c_kv = _rms_rows(u[:, C_CKV:C_KR], gkv_ref[0]).astype(bf16)
    kv = _dot(c_kv, wkv_ref[0])
    kr = u[:, C_KR:C_GQ]
    kr = kr * cos_a + _swap_halves_64(kr) * sin_a
    rope_base = MLA_HEADS * NOPE_DIM
    for pair in range(MLA_HEADS // 2):
        blk = qa[:, rope_base + LANES * pair:rope_base + LANES * (pair + 1)]
        rot = blk * cos_a + _swap_halves_64(blk) * sin_a
        for half in range(2):
            h = 2 * pair + half
            own = jnp.where((lane >> 6) == half, rot, 0.0)
            qh = jnp.concatenate([qa[:, NOPE_DIM * h:NOPE_DIM * (h + 1)], own], axis=1) * MLA_SCALE
            q_mla_ref[0, h] = qh.astype(bf16)
    for h in range(MLA_HEADS):
        kh = jnp.concatenate([kv[:, NOPE_DIM * h:NOPE_DIM * (h + 1)], kr], axis=1)
        k_mla_ref[0, h] = kh.astype(bf16)
        v0 = MLA_HEADS * NOPE_DIM + V_DIM * h
        v_mla_ref[0, h] = kv[:, v0:v0 + V_DIM].astype(bf16)

    for h in range(GQA_HEADS):
        qh = _rms_rows(u[:, C_GQ + GQA_HEAD_DIM * h:C_GQ + GQA_HEAD_DIM * (h + 1)], gqq_ref[0])
        qh = qh * cos_b + pltpu.roll(qh, 64, 1) * sin_b
        q_gqa_ref[0, h] = (qh * GQA_SCALE).astype(bf16)
    for h in range(GQA_KV_HEADS):
        kh = _rms_rows(u[:, C_GK + GQA_HEAD_DIM * h:C_GK + GQA_HEAD_DIM * (h + 1)], gqk_ref[0])
        kh = kh * cos_b + pltpu.roll(kh, 64, 1) * sin_b
        k_gqa_ref[0, h] = kh.astype(bf16)
        v_gqa_ref[0, h] = u[:, C_GV + GQA_HEAD_DIM * h:C_GV + GQA_HEAD_DIM * (h + 1)].astype(bf16)


def _in_proj(hb, w_in_p, wq_p, wkv_p, g_q, g_kv, g_qq, g_qk, tabs, layer, batch, tp):
    n, d = hb.shape
    tm = _row_tile(tp, IN_PROJ_ROWS)
    nt = tp // tm
    cos_a, sin_a, cos_b, sin_b = tabs
    wmap = lambda b, i: (layer, 0, 0)
    tmap = lambda b, i: (i, 0)
    omap = lambda b, i: (b, 0, i, 0)
    bf16 = jnp.bfloat16
    return pl.pallas_call(
        _in_proj_kernel,
        grid=(batch, nt),
        in_specs=[
            pl.BlockSpec((tm, d), lambda b, i: (b * nt + i, 0)),
            _const_spec((1, d, IN_COLS_PAD), wmap),
            _const_spec((1, Q_LORA, wq_p.shape[2]), wmap),
            _const_spec((1, KV_LORA, wkv_p.shape[2]), wmap),
            _const_spec((1, 1, Q_LORA), wmap),
            _const_spec((1, 1, KV_LORA), wmap),
            _const_spec((1, 1, GQA_HEAD_DIM), wmap),
            _const_spec((1, 1, GQA_HEAD_DIM), wmap),
            pl.BlockSpec((tm, LANES), tmap),
            pl.BlockSpec((tm, LANES), tmap),
            pl.BlockSpec((tm, LANES), tmap),
            pl.BlockSpec((tm, LANES), tmap),
        ],
        out_specs=[
            pl.BlockSpec((1, MLA_HEADS, tm, MLA_QK_PAD), omap),
            pl.BlockSpec((1, MLA_HEADS, tm, MLA_QK_PAD), omap),
            pl.BlockSpec((1, MLA_HEADS, tm, V_DIM), omap),
            pl.BlockSpec((1, GQA_HEADS, tm, GQA_HEAD_DIM), omap),
            pl.BlockSpec((1, GQA_KV_HEADS, tm, GQA_HEAD_DIM), omap),
            pl.BlockSpec((1, GQA_KV_HEADS, tm, GQA_HEAD_DIM), omap),
        ],
        out_shape=[
            jax.ShapeDtypeStruct((batch, MLA_HEADS, tp, MLA_QK_PAD), bf16),
            jax.ShapeDtypeStruct((batch, MLA_HEADS, tp, MLA_QK_PAD), bf16),
            jax.ShapeDtypeStruct((batch, MLA_HEADS, tp, V_DIM), bf16),
            jax.ShapeDtypeStruct((batch, GQA_HEADS, tp, GQA_HEAD_DIM), bf16),
            jax.ShapeDtypeStruct((batch, GQA_KV_HEADS, tp, GQA_HEAD_DIM), bf16),
            jax.ShapeDtypeStruct((batch, GQA_KV_HEADS, tp, GQA_HEAD_DIM), bf16),
        ],
        compiler_params=_cparams(("parallel", "parallel")),
        name="in_proj",
    )(hb, w_in_p, wq_p, wkv_p, g_q, g_kv, g_qq, g_qk, cos_a, sin_a, cos_b, sin_b)


def _attn_kernel(q_ref, k_ref, v_ref, *rest, tq, n_keys, n_cast):
    o_ref = rest[n_cast]
    for src_ref, dst_ref in zip(rest[:n_cast], rest[n_cast + 1:]):
        dst_ref[...] = src_ref[...].astype(dst_ref.dtype)
    tp = k_ref.shape[2]
    body = tp - LANES
    nt = (((1,), (1,)), ((), ()))
    k_body, k_tail = k_ref[0, 0, :body, :], k_ref[0, 0, body:, :]
    v_body, v_tail = v_ref[0, 0, :body, :], v_ref[0, 0, body:, :]
    tail_ok = lax.broadcasted_iota(jnp.int32, (1, LANES), 1) < (n_keys - body)
    n_q = -(-n_keys // BF16_ROWS) * BF16_ROWS
    if n_q < tp:
        o_ref[n_q:, :] = jnp.zeros((tp - n_q, o_ref.shape[1]), o_ref.dtype)
    for r0 in range(0, n_q, tq):
        rows = slice(r0, min(r0 + tq, n_q))
        q = q_ref[0, 0, rows, :]
        s_body = lax.dot_general(q, k_body, nt, preferred_element_type=jnp.float32)
        s_tail = lax.dot_general(q, k_tail, nt, preferred_element_type=jnp.float32)
        s_tail = jnp.where(tail_ok, s_tail, NEG_BIG)
        m = jnp.maximum(jnp.max(s_body, axis=-1, keepdims=True), jnp.max(s_tail, axis=-1, keepdims=True))
        p_body = jnp.exp2(s_body - m)
        p_tail = jnp.exp2(s_tail - m)
        l = jnp.sum(p_body, axis=-1, keepdims=True) + jnp.sum(p_tail, axis=-1, keepdims=True)
        o = _dot(p_body.astype(jnp.bfloat16), v_body) + _dot(p_tail.astype(jnp.bfloat16), v_tail)
        o_ref[rows, :] = (o / l).astype(o_ref.dtype)


def _attention(q, k, v, n_keys, casts):
    batch, hq, tp, dk = q.shape
    hk, dv = k.shape[1], v.shape[3]
    rep = hq // hk
    tq = _row_tile(tp, ATTN_Q_ROWS)
    steps = batch * hq
    cast_in, cast_out, cast_shapes = [], [], []
    for w, layer in casts:
        rows = w.shape[0] // (layer[1])
        assert rows % steps == 0
        blk = rows // steps
        off = layer[0] * steps
        cast_in.append(pl.BlockSpec((blk, w.shape[1]), lambda b, h, off=off: (off + b * hq + h, 0)))
        cast_out.append(pl.BlockSpec((blk, w.shape[1]), lambda b, h: (b * hq + h, 0)))
        cast_shapes.append(jax.ShapeDtypeStruct((rows, w.shape[1]), jnp.bfloat16))
    outs = pl.pallas_call(
        functools.partial(_attn_kernel, tq=tq, n_keys=n_keys, n_cast=len(casts)),
        grid=(batch, hq),
        in_specs=[
            pl.BlockSpec((1, 1, tp, dk), lambda b, h: (b, h, 0, 0)),
            pl.BlockSpec((1, 1, tp, dk), lambda b, h: (b, h // rep, 0, 0)),
            pl.BlockSpec((1, 1, tp, dv), lambda b, h: (b, h // rep, 0, 0)),
        ] + cast_in,
        out_specs=[pl.BlockSpec((tp, dv), lambda b, h: (b, h))] + cast_out,
        out_shape=[jax.ShapeDtypeStruct((batch * tp, hq * dv), jnp.bfloat16)] + cast_shapes,
        compiler_params=_cparams(("parallel", "parallel")),
        name="attention",
    )(q, k, v, *[w for w, _ in casts])
    return outs[0], outs[1:]


def _first_index_of_max(vals, lane_f):
    m = jnp.max(vals, axis=-1, keepdims=True)
    idx = jnp.min(jnp.where(vals == m, lane_f, float(LANES)), axis=-1, keepdims=True)
    return m, idx


def _out_proj_kernel(oa_ref, ob_ref, h_ref, w_out_ref, ga_ref, gb_ref, lg_ref, lb_ref,
                     wr_hl_ref, rbias_ref,
                     h1_ref, h1p_ref, ri_ref, cnt_ref, carry_ref, *, alpha, tp, n_tok, sub):
    i = pl.program_id(0)
    tm = h_ref.shape[0]

    @pl.when(i == 0)
    def _():
        carry_ref[...] = jnp.zeros_like(carry_ref)

    blocks = [slice(r0, r0 + sub) for r0 in range(0, tm, sub)]
    mixed = [_out_proj_matmul(rows, oa_ref, ob_ref, w_out_ref, ga_ref, gb_ref) for rows in blocks]
    routed_in = [_out_proj_norm(rows, m, h_ref, lg_ref, lb_ref, wr_hl_ref, h1_ref, alpha=alpha)
                 for rows, m in zip(blocks, mixed)]
    for rows, (h1, logits) in zip(blocks, routed_in):
        _out_proj_route(rows, i * tm + rows.start, h1, logits, rbias_ref, h1p_ref, ri_ref, carry_ref,
                        tp=tp, n_tok=n_tok)
    cnt_ref[...] = carry_ref[...]


def _out_proj_matmul(rows, oa_ref, ob_ref, w_out_ref, ga_ref, gb_ref):
    bf16 = jnp.bfloat16
    na = _rms_rows(oa_ref[rows, :].astype(jnp.float32), ga_ref[0]).astype(bf16)
    nb = _rms_rows(ob_ref[rows, :].astype(jnp.float32), gb_ref[0]).astype(bf16)
    return _dot(na, w_out_ref[0, :MLA_WIDTH, :]) + _dot(nb, w_out_ref[0, MLA_WIDTH:, :])


def _out_proj_norm(rows, mixed, h_ref, lg_ref, lb_ref, wr_hl_ref, h1_ref, *, alpha):
    bf16 = jnp.bfloat16
    h1 = _layer_norm_rows(alpha * h_ref[rows, :] + mixed, lg_ref[0], lb_ref[0])
    h1_ref[rows, :] = h1
    hi = h1.astype(bf16)
    lo = (h1 - hi.astype(jnp.float32)).astype(bf16)
    hi_prod = _dot(hi, wr_hl_ref[...])
    logits = hi_prod[:, :LANES] + hi_prod[:, LANES:] + _dot(lo, wr_hl_ref[:, :LANES])
    return h1, logits


def _out_proj_route(rows, row0, h1, logits, rbias_ref, h1p_ref, ri_ref, carry_ref, *, tp, n_tok):
    bf16 = jnp.bfloat16
    f32 = jnp.float32
    tm = rows.stop - rows.start
    half = h1.shape[1] // 2
    scores = jax.nn.sigmoid(logits)

    lane = lax.broadcasted_iota(jnp.int32, (tm, LANES), 1)
    lane_f = lane.astype(f32)
    neg = -jnp.inf
    sel = jnp.where(lane < N_EXPERTS, scores + rbias_ref[...], neg)
    grp = lane >> 2

    best = None
    for g in range(N_GROUPS):
        mg = jnp.where(grp == g, sel, neg)
        m1, i1 = _first_index_of_max(mg, lane_f)
        m2 = jnp.max(jnp.where(lane_f == i1, neg, mg), axis=-1, keepdims=True)
        gs = m1 + m2
        if best is None:
            best, gi = gs, jnp.zeros_like(gs)
        else:
            better = gs > best
            gi = jnp.where(better, float(g), gi)
            best = jnp.where(better, gs, best)

    mg = jnp.where(grp.astype(f32) == gi, sel, neg)
    _, e1 = _first_index_of_max(mg, lane_f)
    mg2 = jnp.where(lane_f == e1, neg, mg)
    _, e2 = _first_index_of_max(mg2, lane_f)
    w1 = jnp.sum(jnp.where(lane_f == e1, scores, 0.0), axis=-1, keepdims=True)
    w2 = jnp.sum(jnp.where(lane_f == e2, scores, 0.0), axis=-1, keepdims=True)
    den = w1 + w2
    w1, w2 = w1 / den, w2 / den

    first_lower = e1 < e2
    la = jnp.where(first_lower, e1, e2) - EXPERTS_PER_GROUP * gi
    lb = jnp.where(first_lower, e2, e1) - EXPERTS_PER_GROUP * gi
    w_a = jnp.where(first_lower, w1, w2)
    w_b = jnp.where(first_lower, w2, w1)
    cls = PAIRS_PER_GROUP * gi + la * (7.0 - la) * 0.5 + (lb - la - 1.0)

    row = (row0 + lax.broadcasted_iota(jnp.int32, (tm, 1), 0)).astype(f32)
    routed = (row - jnp.floor((row + 0.5) * (1.0 / tp)) * tp) < n_tok
    onehot = jnp.where((lane_f == cls) & routed, 1.0, 0.0)

    r_i = lax.broadcasted_iota(jnp.int32, (tm, tm), 0)
    c_i = lax.broadcasted_iota(jnp.int32, (tm, tm), 1)
    lower = jnp.where(c_i < r_i, 1.0, 0.0).astype(bf16)
    before = _dot(lower, onehot.astype(bf16)) + carry_ref[...]
    rank = jnp.sum(onehot * before, axis=-1, keepdims=True)
    carry_ref[...] += jnp.sum(onehot, axis=0, keepdims=True)

    cls_out = jnp.where(routed, cls, -1.0)
    ri_ref[rows, :] = jnp.where(lane == 0, cls_out, jnp.where(lane == 1, rank, 0.0)).astype(jnp.int32)

    h1p_ref[rows, :half] = _pack_bf16_pair(h1[:, :half], h1[:, half:])
    weights = jnp.where(lane == 0, w_a, jnp.where(lane == 1, w_b, 0.0))
    h1p_ref[rows, half:] = lax.bitcast_convert_type(weights, jnp.int32)


def _out_proj(o_a, o_b, h, w_out_b, g_a, g_b, ln_g, ln_b, wr_hl, rbias, layer, alpha, tp, n_tok):
    n, d = h.shape
    sub = _row_tile(n, OUT_PROJ_SUB_ROWS)
    tm = 2 * sub if n % (2 * sub) == 0 else sub
    row_w = d // 2 + LANES
    wmap = lambda i: (layer, 0, 0)
    cmap = lambda i: (0, 0)
    rmap = lambda i: (i, 0)
    return pl.pallas_call(
        functools.partial(_out_proj_kernel, alpha=alpha, tp=tp, n_tok=n_tok, sub=sub),
        grid=(n // tm,),
        in_specs=[
            pl.BlockSpec((tm, MLA_WIDTH), rmap),
            pl.BlockSpec((tm, GQA_WIDTH), rmap),
            pl.BlockSpec((tm, d), rmap),
            _const_spec((1, MLA_WIDTH + GQA_WIDTH, d), wmap),
            _const_spec((1, 1, MLA_WIDTH), wmap),
            _const_spec((1, 1, GQA_WIDTH), wmap),
            _const_spec((1, 1, d), wmap),
            _const_spec((1, 1, d), wmap),
            _const_spec((d, 2 * LANES), cmap),
            _const_spec((1, LANES), cmap),
        ],
        out_specs=[
            pl.BlockSpec((tm, d), rmap),
            pl.BlockSpec((tm, row_w), rmap),
            pl.BlockSpec((tm, LANES), rmap),
            pl.BlockSpec((1, LANES), cmap),
        ],
        out_shape=[
            jax.ShapeDtypeStruct((n, d), jnp.float32),
            jax.ShapeDtypeStruct((n, row_w), jnp.int32),
            jax.ShapeDtypeStruct((n, LANES), jnp.int32),
            jax.ShapeDtypeStruct((1, LANES), jnp.float32),
        ],
        scratch_shapes=[pltpu.VMEM((1, LANES), jnp.float32)],
        compiler_params=_cparams(("arbitrary",)),
        name="out_proj",
    )(o_a, o_b, h, w_out_b, g_a, g_b, ln_g, ln_b, wr_hl, rbias)


def _start_row_copies(idx_ref, n_rows, make_copy):
    def issue(g, carry):
        for k in range(SUBLANES):
            idx = idx_ref[0, 0, g * SUBLANES + k]
            make_copy(g, k, lax.shift_right_logical(idx, 3), idx & (SUBLANES - 1)).start(priority=k % 2)
        return carry

    lax.fori_loop(0, n_rows // SUBLANES, issue, 0)


def _dispatch_kernel(pos_ref, x_ref, xs_init_hbm, xs_hbm, buf_ref, sem):
    del xs_init_hbm
    i = pl.program_id(0)
    last = pl.num_programs(0) - 1
    slot = i % 2
    tn = x_ref.shape[0]
    groups = tn // SUBLANES

    def wait_slot(s):
        pltpu.make_async_copy(buf_ref.at[s], xs_hbm.at[pl.ds(0, groups)], sem.at[s]).wait()

    @pl.when(i >= 2)
    def _():
        wait_slot(slot)

    buf_ref[slot] = x_ref[...].reshape(groups, SUBLANES, x_ref.shape[1])
    _start_row_copies(pos_ref, tn, lambda g, k, hi, lo: pltpu.make_async_copy(
        buf_ref.at[slot, g, pl.ds(k, 1), :], xs_hbm.at[hi, pl.ds(lo, 1), :], sem.at[slot]))

    @pl.when(i == last)
    def _():
        wait_slot(slot)

        @pl.when(last >= 1)
        def _():
            wait_slot(1 - slot)


def _dispatch(pos3, h1p, xs_init):
    n, row_w = h1p.shape
    tn = pos3.shape[2]
    return pl.pallas_call(
        _dispatch_kernel,
        grid=(n // tn,),
        in_specs=[
            pl.BlockSpec((1, 1, tn), lambda i: (i, 0, 0), memory_space=pltpu.SMEM),
            pl.BlockSpec((tn, row_w), lambda i: (i, 0)),
            pl.BlockSpec(memory_space=pl.ANY),
        ],
        out_specs=pl.BlockSpec(memory_space=pl.ANY),
        out_shape=jax.ShapeDtypeStruct(xs_init.shape, xs_init.dtype),
        scratch_shapes=[pltpu.VMEM((2, tn // SUBLANES, SUBLANES, row_w), h1p.dtype),
                        pltpu.SemaphoreType.DMA((2,))],
        input_output_aliases={2: 0},
        compiler_params=_cparams(("arbitrary",)),
        name="dispatch",
    )(pos3, h1p, xs_init)


def _experts_kernel(plan_ref, meta_ref, xs_ref, wg_hbm, wu_hbm, wd_hbm, ys_ref, acc_ref, wg_buf, wu_buf, wd_buf,
                    sem):
    i = pl.program_id(0)
    k = pl.program_id(1)
    steps = 2 * pl.num_programs(0)
    t = 2 * i + k
    half = wg_buf.shape[1] // 2

    def weight_copies(e, s):
        return (pltpu.make_async_copy(wg_hbm.at[e], wg_buf.at[s], sem.at[s]),
                pltpu.make_async_copy(wu_hbm.at[e], wu_buf.at[s], sem.at[s]),
                pltpu.make_async_copy(wd_hbm.at[e], wd_buf.at[s], sem.at[s]))

    @pl.when(i < meta_ref[0])
    def _():
        slot = plan_ref[2 * steps + t]

        @pl.when(plan_ref[steps + t] == 1)
        def _():
            @pl.when(t == 0)
            def _():
                for c in weight_copies(plan_ref[0], slot):
                    c.start()

            for c in weight_copies(plan_ref[t], slot):
                c.wait()
            nxt = plan_ref[3 * steps + t]

            @pl.when(nxt >= 0)
            def _():
                for c in weight_copies(nxt, 1 - slot):
                    c.start()

        lo, hi = _unpack_bf16_pair(xs_ref[:, :half])
        g = _dot(lo, wg_buf[slot, :half, :]) + _dot(hi, wg_buf[slot, half:, :])
        u = _dot(lo, wu_buf[slot, :half, :]) + _dot(hi, wu_buf[slot, half:, :])
        a = (g * jax.nn.sigmoid(g) * u).astype(jnp.bfloat16)
        y = _dot(a, wd_buf[slot])
        second = (k + i) % 2 == 1
        wab = lax.bitcast_convert_type(xs_ref[:, half:], jnp.float32)
        y = y * jnp.where(second, wab[:, 1:2], wab[:, 0:1])

        @pl.when(k == 0)
        def _():
            acc_ref[...] = y

        @pl.when(k == 1)
        def _():
            total = acc_ref[...] + y
            ys_ref[...] = _pack_bf16_pair(total[:, :half], total[:, half:])

    @pl.when((i >= meta_ref[0]) & (k == 1))
    def _():
        ys_ref[...] = jnp.zeros_like(ys_ref)


def _experts(plan, meta, xs, wg_b, wu_b, wd_b, max_tiles):
    tm = EXPERT_TILE
    row_w = xs.shape[1]
    d, ff = wg_b.shape[1], wg_b.shape[2]

    def xmap(i, k, pr, mt):
        return (jnp.minimum(i, mt[0] - 1), 0)

    grid_spec = pltpu.PrefetchScalarGridSpec(
        num_scalar_prefetch=2,
        grid=(max_tiles, 2),
        in_specs=[
            pl.BlockSpec((tm, row_w), xmap),
            pl.BlockSpec(memory_space=pl.ANY),
            pl.BlockSpec(memory_space=pl.ANY),
            pl.BlockSpec(memory_space=pl.ANY),
        ],
        out_specs=pl.BlockSpec((tm, d // 2), lambda i, k, pr, mt: (i, 0)),
        scratch_shapes=[
            pltpu.VMEM((tm, d), jnp.float32),
            pltpu.VMEM((2, d, ff), wg_b.dtype),
            pltpu.VMEM((2, d, ff), wu_b.dtype),
            pltpu.VMEM((2, ff, d), wd_b.dtype),
            pltpu.SemaphoreType.DMA((2,)),
        ],
    )
    return pl.pallas_call(
        _experts_kernel,
        grid_spec=grid_spec,
        out_shape=jax.ShapeDtypeStruct((max_tiles * tm, d // 2), jnp.int32),
        compiler_params=_cparams(("arbitrary", "arbitrary")),
        name="experts",
    )(plan, meta, xs, wg_b, wu_b, wd_b)


def _combine_kernel(pos_ref, pos_next_ref, ys_hbm, h1_ref, g_ref, b_ref, *rest, alpha, final):
    if final:
        out_ref, ybuf_ref, sem = rest
        i = pl.program_id(0) * pl.num_programs(1) + pl.program_id(1)
        last = pl.num_programs(0) * pl.num_programs(1) - 1
    else:
        h_ref, hb_ref, ybuf_ref, sem = rest
        i = pl.program_id(0)
        last = pl.num_programs(0) - 1
    slot = i % 2
    tn = h1_ref.shape[0]

    def start_gather(idx_ref, s):
        _start_row_copies(idx_ref, tn, lambda g, k, hi, lo: pltpu.make_async_copy(
            ys_hbm.at[hi, pl.ds(lo, 1), :], ybuf_ref.at[s, g, pl.ds(k, 1), :], sem.at[s]))

    @pl.when(i == 0)
    def _():
        start_gather(pos_ref, 0)

    pltpu.make_async_copy(ys_hbm.at[pl.ds(0, tn // SUBLANES)], ybuf_ref.at[slot], sem.at[slot]).wait()

    @pl.when(i < last)
    def _():
        start_gather(pos_next_ref, 1 - slot)

    lo, hi = _unpack_bf16_pair(ybuf_ref[slot].reshape(tn, h1_ref.shape[1] // 2), jnp.float32)
    y = _layer_norm_rows(alpha * h1_ref[...] + jnp.concatenate([lo, hi], axis=1), g_ref[0], b_ref[0])
    if final:
        out_ref[0] = y
    else:
        h_ref[...] = y
        hb_ref[...] = y.astype(jnp.bfloat16)


def _combine_final(pos, ys, h1, ln_g, ln_b, layer, alpha, batch, seq, tp):
    n, d = h1.shape
    tn = tp - seq
    per_batch = tp // tn
    nx = seq // tn
    pos3 = pos.reshape(batch * per_batch, 1, tn)
    wmap = lambda b, j: (layer, 0, 0)
    cur = lambda b, j: b * per_batch + j

    def nxt(b, j):
        wrap = j + 1 >= nx
        return jnp.where(wrap, jnp.minimum(b + 1, batch - 1) * per_batch, b * per_batch + j + 1)

    return pl.pallas_call(
        functools.partial(_combine_kernel, alpha=alpha, final=True),
        grid=(batch, nx),
        in_specs=[
            pl.BlockSpec((1, 1, tn), lambda b, j: (cur(b, j), 0, 0), memory_space=pltpu.SMEM),
            pl.BlockSpec((1, 1, tn), lambda b, j: (nxt(b, j), 0, 0), memory_space=pltpu.SMEM),
            pl.BlockSpec(memory_space=pl.ANY),
            pl.BlockSpec((tn, d), lambda b, j: (cur(b, j), 0)),
            _const_spec((1, 1, d), wmap),
            _const_spec((1, 1, d), wmap),
        ],
        out_specs=pl.BlockSpec((1, tn, d), lambda b, j: (b, j, 0)),
        out_shape=jax.ShapeDtypeStruct((batch, seq, d), jnp.float32),
        scratch_shapes=[pltpu.VMEM((2, tn // SUBLANES, SUBLANES, d // 2), jnp.int32),
                        pltpu.SemaphoreType.DMA((2,))],
        compiler_params=_cparams(("arbitrary", "arbitrary")),
        name="combine_final",
    )(pos3, pos3, ys, h1, ln_g, ln_b)


def _combine(pos, ys, h1, ln_g, ln_b, layer, alpha, tn):
    n, d = h1.shape
    steps = n // tn
    pos3 = pos.reshape(steps, 1, tn)
    wmap = lambda i: (layer, 0, 0)
    rmap = lambda i: (i, 0)
    return pl.pallas_call(
        functools.partial(_combine_kernel, alpha=alpha, final=False),
        grid=(steps,),
        in_specs=[
            pl.BlockSpec((1, 1, tn), lambda i: (i, 0, 0), memory_space=pltpu.SMEM),
            pl.BlockSpec((1, 1, tn), lambda i: (jnp.minimum(i + 1, steps - 1), 0, 0), memory_space=pltpu.SMEM),
            pl.BlockSpec(memory_space=pl.ANY),
            pl.BlockSpec((tn, d), rmap),
            _const_spec((1, 1, d), wmap),
            _const_spec((1, 1, d), wmap),
        ],
        out_specs=[pl.BlockSpec((tn, d), rmap), pl.BlockSpec((tn, d), rmap)],
        out_shape=[jax.ShapeDtypeStruct((n, d), jnp.float32), jax.ShapeDtypeStruct((n, d), jnp.bfloat16)],
        scratch_shapes=[pltpu.VMEM((2, tn // SUBLANES, SUBLANES, d // 2), jnp.int32),
                        pltpu.SemaphoreType.DMA((2,))],
        compiler_params=_cparams(("arbitrary",)),
        name="combine",
    )(pos3, pos3, ys, h1, ln_g, ln_b)


def _rope_tables(seq, tp):
    rows = seq // GRID_W
    pad = tp - seq - N_META
    pos_row = jnp.concatenate([jnp.repeat(jnp.arange(rows, dtype=jnp.float32), GRID_W),
                               jnp.full((N_META,), -1.0, jnp.float32), jnp.zeros((pad,), jnp.float32)])
    pos_col = jnp.concatenate([jnp.tile(jnp.arange(GRID_W, dtype=jnp.float32), rows),
                               jnp.arange(N_META, dtype=jnp.float32), jnp.zeros((pad,), jnp.float32)])

    def tables(rot_dim):
        n = rot_dim // 4
        inv = ROPE_THETA ** (-jnp.arange(n, dtype=jnp.float32) / n)
        ang = jnp.concatenate([pos_row[:, None] * inv, pos_col[:, None] * inv], axis=-1)
        cos, sin = jnp.cos(ang), jnp.sin(ang)
        reps = LANES // rot_dim
        return (jnp.tile(jnp.concatenate([cos, cos], axis=-1), (1, reps)),
                jnp.tile(jnp.concatenate([-sin, sin], axis=-1), (1, reps)))

    cos_a, sin_a = tables(ROPE_DIM)
    cos_b, sin_b = tables(GQA_HEAD_DIM)
    return cos_a, sin_a, cos_b, sin_b


def _relayout_weights(w_in, w_q_b, w_kv_b):
    bf16 = jnp.bfloat16
    off_kr = Q_LORA + KV_LORA
    off_gq = off_kr + ROPE_DIM
    kr = w_in[:, :, off_kr:off_gq]
    w_in_p = jnp.concatenate([w_in[:, :, :off_kr], kr, kr, w_in[:, :, off_gq:]], axis=-1).astype(bf16)
    depth = w_in.shape[0]
    wq = w_q_b.reshape(depth, Q_LORA, MLA_HEADS, QK_HEAD)
    wq_p = jnp.concatenate([wq[..., :NOPE_DIM].reshape(depth, Q_LORA, -1),
                            wq[..., NOPE_DIM:].reshape(depth, Q_LORA, -1)], axis=-1).astype(bf16)
    wkv = w_kv_b.reshape(depth, KV_LORA, MLA_HEADS, NOPE_DIM + V_DIM)
    wkv_p = jnp.concatenate([wkv[..., :NOPE_DIM].reshape(depth, KV_LORA, -1),
                             wkv[..., NOPE_DIM:].reshape(depth, KV_LORA, -1)], axis=-1).astype(bf16)
    return w_in_p, wq_p, wkv_p


def _dispatch_plan(ri, counts, max_tiles, tp, n_tok):
    tm = EXPERT_TILE
    n = ri.shape[0]
    rows = max_tiles * tm
    cls, rank = ri[:, 0], ri[:, 1]
    cnt = counts[0, :N_CLASSES].astype(jnp.int32)
    tiles_c = (cnt + tm - 1) // tm
    tile_end = jnp.cumsum(tiles_c)
    tile_start = tile_end - tiles_c
    n_tiles = tile_end[-1]
    routed = cls >= 0
    tok = np.arange(n, dtype=np.int32)
    spare = jnp.asarray(rows + (tok // tp) * (tp - n_tok) + (tok % tp - n_tok), dtype=jnp.int32)
    in_class = cls[None, :] == jnp.arange(N_CLASSES, dtype=jnp.int32)[:, None]
    sorted_pos = jnp.sum(jnp.where(in_class, (tile_start * tm)[:, None], 0), axis=0) + rank
    pos_scatter = jnp.where(routed, sorted_pos, spare).astype(jnp.int32)
    pos_gather = jnp.where(routed, sorted_pos, 0).astype(jnp.int32)

    t = jnp.minimum(jnp.arange(max_tiles, dtype=jnp.int32), n_tiles - 1)
    tile_cls = jnp.sum((t[:, None] >= tile_end[None, :]).astype(jnp.int32), axis=1)
    tile_cls = jnp.minimum(tile_cls, N_CLASSES - 1)
    pair_lo = jnp.array([0, 0, 0, 1, 1, 2], jnp.int32)
    pair_hi = jnp.array([1, 2, 3, 2, 3, 3], jnp.int32)
    e_a = EXPERTS_PER_GROUP * (tile_cls // PAIRS_PER_GROUP) + pair_lo[tile_cls % PAIRS_PER_GROUP]
    e_b = EXPERTS_PER_GROUP * (tile_cls // PAIRS_PER_GROUP) + pair_hi[tile_cls % PAIRS_PER_GROUP]
    odd = (jnp.arange(max_tiles) % 2) == 1
    tile_e = jnp.stack([jnp.where(odd, e_b, e_a), jnp.where(odd, e_a, e_b)], axis=-1).reshape(-1)
    step_tile = jnp.arange(2 * max_tiles) // 2
    tile_e = jnp.where(step_tile >= n_tiles, tile_e[2 * n_tiles - 1], tile_e).astype(jnp.int32)
    n_steps = 2 * max_tiles
    step = jnp.arange(n_steps, dtype=jnp.int32)
    used = step_tile < n_tiles
    prev_e = jnp.concatenate([jnp.full((1,), -1, jnp.int32), tile_e[:-1]])
    run_start = used & (tile_e != prev_e)
    slot = jnp.cumsum(run_start.astype(jnp.int32)) % 2
    later_start = run_start[None, :] & (step[None, :] > step[:, None])
    next_start = jnp.min(jnp.where(later_start, step[None, :], n_steps), axis=1)
    next_e = jnp.where(next_start < n_steps, tile_e[jnp.minimum(next_start, n_steps - 1)], -1)
    plan = jnp.concatenate([tile_e, run_start.astype(jnp.int32), slot, next_e]).astype(jnp.int32)
    meta = n_tiles.reshape(1).astype(jnp.int32)
    return plan, meta, pos_scatter, pos_gather


def kernel(x, meta_tokens, ln_in_g, ln_in_b, w_in, g_q_lora, w_q_b, g_kv_lora, w_kv_b, g_qk_q, g_qk_k,
           g_out_mla, g_out_gqa, w_out, ln1_g, ln1_b, w_router, router_bias, w_gate, w_up, w_down,
           ln2_g, ln2_b):
    batch, seq, d = x.shape
    depth = w_in.shape[0]
    n_tok = seq + N_META
    tp = -(-n_tok // LANES) * LANES
    n = batch * tp
    alpha = (2.0 * depth) ** 0.25
    bf16 = jnp.bfloat16
    f32 = jnp.float32

    tabs = _rope_tables(seq, tp)
    w_in_p, wq_p, wkv_p = _relayout_weights(w_in, w_q_b, w_kv_b)
    w_out_b = w_out.astype(bf16)
    n_exp, ff = w_gate.shape[1], w_gate.shape[3]
    wg_rows = w_gate.reshape(depth * n_exp * d, ff)
    wu_rows = w_up.reshape(depth * n_exp * d, ff)
    wd_rows = w_down.reshape(depth * n_exp * ff, d)
    wr = jnp.pad(w_router.astype(f32), ((0, 0), (0, LANES - N_EXPERTS)))
    wr_hi = wr.astype(bf16)
    wr_lo = (wr - wr_hi.astype(f32)).astype(bf16)
    wr_hl = jnp.concatenate([wr_hi, wr_lo], axis=1)
    rbias = jnp.pad(router_bias.astype(f32), (0, LANES - N_EXPERTS)).reshape(1, LANES)
    row3 = lambda a: a.reshape(depth, 1, a.shape[-1])

    h, hb = _ln_in(x, meta_tokens.astype(x.dtype), ln_in_g.reshape(1, d), ln_in_b.reshape(1, d), tp)

    tm_e = EXPERT_TILE
    max_tiles = -(-(batch * n_tok) // tm_e) + N_CLASSES
    spare_rows = -(-(batch * (tp - n_tok)) // tm_e) * tm_e
    tn = _row_tile(n, ROW_DMA_ROWS)
    xs_rows, row_w = max_tiles * tm_e + spare_rows, d // 2 + LANES
    xs = jnp.zeros((xs_rows // SUBLANES, SUBLANES, row_w), jnp.int32)
    for l in range(depth):
        q_a, k_a, v_a, q_b, k_b, v_b = _in_proj(
            hb, w_in_p, wq_p, wkv_p, row3(g_q_lora), row3(g_kv_lora), row3(g_qk_q), row3(g_qk_k),
            tabs, l, batch, tp)
        o_a, (wg_b, wu_b) = _attention(q_a, k_a, v_a, n_tok, [(wg_rows, (l, depth)), (wu_rows, (l, depth))])
        o_b, (wd_b,) = _attention(q_b, k_b, v_b, n_tok, [(wd_rows, (l, depth))])
        wg_b, wu_b = wg_b.reshape(n_exp, d, ff), wu_b.reshape(n_exp, d, ff)
        wd_b = wd_b.reshape(n_exp, ff, d)
        h1, h1p, ri, counts = _out_proj(
            o_a, o_b, h, w_out_b, row3(g_out_mla), row3(g_out_gqa), row3(ln1_g), row3(ln1_b),
            wr_hl, rbias, l, alpha, tp, n_tok)
        tile_e, tmeta, pos_scatter, pos_gather = _dispatch_plan(ri, counts, max_tiles, tp, n_tok)
        xs = _dispatch(pos_scatter.reshape(n // tn, 1, tn), h1p, xs)
        ys = _experts(tile_e, tmeta, xs.reshape(xs_rows, row_w), wg_b, wu_b, wd_b, max_tiles)
        ys = ys.reshape(ys.shape[0] // SUBLANES, SUBLANES, d // 2)
        if l + 1 < depth:
            h, hb = _combine(pos_gather, ys, h1, row3(ln2_g), row3(ln2_b), l, alpha, tn)
        else:
            out = _combine_final(pos_gather, ys, h1, row3(ln2_g), row3(ln2_b), l, alpha, batch, seq, tp)
    return out
```
